```python
import jax, jax.numpy as jnp
from jax import lax
import numpy as np

D_MODEL = 4096
BATCH = 2
SEQ = 8192
DEPTH = 4

CTX_LEN = 256
GRID_W = 64
N_BRANCH = 4
W_BR = D_MODEL // N_BRANCH
W_A = W_BR
W_B = W_BR
W_C = W_BR
W_D = W_BR
LRU_HEADS = 16
LRU_HEAD_DIM = W_A // LRU_HEADS
LRU_C = 8.0
LRU_CONV = 4
CF_CONV = 31
SC_CONV = 3
FNET_GROUPS = 4
EPS = 1e-6
N_IN = 2 * W_A + 3 * W_B + 4 * W_C + 2 * W_D + N_BRANCH * D_MODEL

kernel_name = "hybrid_lru_conformer_shortconv_fnet_diffusion_trunk"


def _split_points():
    sizes = (W_A, 2 * W_B, 3 * W_C, W_D, W_A, W_B, W_C, W_D) + (D_MODEL,) * N_BRANCH
    pts, s = [], 0
    for n in sizes[:-1]:
        s += n
        pts.append(s)
    return pts


def _rmsnorm(v, g):
    v32 = v.astype(jnp.float32)
    n = v32 * lax.rsqrt(jnp.mean(v32 * v32, axis=-1, keepdims=True) + EPS)
    return (n * g.astype(jnp.float32)).astype(v.dtype)


def _layernorm(v, g, b):
    v32 = v.astype(jnp.float32)
    mu = jnp.mean(v32, axis=-1, keepdims=True)
    d = v32 - mu
    var = jnp.mean(d * d, axis=-1, keepdims=True)
    return (d * lax.rsqrt(var + EPS) * g.astype(jnp.float32) + b.astype(jnp.float32)).astype(v.dtype)


def _dwconv(v, w, pad_l, pad_r):
    return lax.conv_general_dilated(
        v, w.astype(v.dtype)[:, None, :], window_strides=(1,), padding=[(pad_l, pad_r)],
        dimension_numbers=("NWC", "WIO", "NWC"), feature_group_count=v.shape[-1])


def _lru_combine(earlier, later):
    a_e, b_e = earlier
    a_l, b_l = later
    return a_e * a_l, a_l * b_e + b_l


def _rglru_scan(u, wa, ba, wx, bx, lam, h0, reverse):
    bsz, n, w = u.shape
    uh = u.reshape(bsz, n, LRU_HEADS, LRU_HEAD_DIM)
    r = jax.nn.sigmoid((jnp.einsum("bnhd,hde->bnhe", uh, wa).reshape(bsz, n, w) + ba).astype(jnp.float32))
    i = jax.nn.sigmoid((jnp.einsum("bnhd,hde->bnhe", uh, wx).reshape(bsz, n, w) + bx).astype(jnp.float32))
    log_a = LRU_C * r * jax.nn.log_sigmoid(lam.astype(jnp.float32))
    a = jnp.exp(log_a)
    b = jnp.sqrt(-jnp.expm1(2.0 * log_a)) * (i * u.astype(jnp.float32))
    if h0 is not None:
        edge = -1 if reverse else 0
        b = b.at[:, edge].add(a[:, edge] * h0)
    _, h = lax.associative_scan(_lru_combine, (a, b), axis=1, reverse=reverse)
    return h


def _rglru_mixer(xa, p, h0_f, h0_b):
    u_f = _dwconv(xa, p["lru_conv_w"][0], LRU_CONV - 1, 0) + p["lru_conv_b"][0]
    u_b = _dwconv(xa, p["lru_conv_w"][1], 0, LRU_CONV - 1) + p["lru_conv_b"][1]
    h_f = _rglru_scan(u_f, p["lru_wa"][0], p["lru_ba"][0], p["lru_wx"][0], p["lru_bx"][0],
                      p["lru_lambda"][0], h0_f, False)
    h_b = _rglru_scan(u_b, p["lru_wa"][1], p["lru_ba"][1], p["lru_wx"][1], p["lru_bx"][1],
                      p["lru_lambda"][1], h0_b, True)
    return h_f, h_b


def _conformer_conv(zb, p):
    val, gl = jnp.split(zb, 2, axis=-1)
    v = val * jax.nn.sigmoid(gl)
    v = _dwconv(v, p["cf_conv_w"], CF_CONV // 2, CF_CONV // 2) + p["cf_conv_b"]
    return jax.nn.silu(_layernorm(v, p["cf_ln_g"], p["cf_ln_b"]))


def _short_conv(zc, p):
    xin, gate_b, gate_c = jnp.split(zc, 3, axis=-1)
    return gate_b * _dwconv(gate_c * xin, p["sc_conv_w"], SC_CONV // 2, SC_CONV // 2)


def _fourier_mix(zd):
    bsz, n, w = zd.shape
    v = zd.astype(jnp.float32).reshape(bsz, n, FNET_GROUPS, w // FNET_GROUPS)
    f = jnp.fft.fftn(v, axes=(1, 3), norm="ortho").real
    return f.reshape(bsz, n, w).astype(zd.dtype)


def _token_mix(z, y_a, p):
    (_, zb, zc, zd, s_a, s_b, s_c, s_d, g_a, g_b, g_c, g_d) = jnp.split(z, _split_points(), axis=-1)
    ys = (y_a, _conformer_conv(zb, p), _short_conv(zc, p), _fourier_mix(zd))
    ss = (s_a, s_b, s_c, s_d)
    gs = (g_a, g_b, g_c, g_d)
    merged = None
    for k in range(N_BRANCH):
        br = (ys[k] * jax.nn.silu(ss[k])) @ p["w_branch"][k]
        term = jax.nn.sigmoid(gs[k]) * br
        merged = term if merged is None else merged + term
    return merged @ p["w_out"]


def setup_inputs(seed: int = 0) -> dict:
    key = jax.random.key(seed)
    ks = jax.random.split(key, 24)
    f32 = jnp.float32

    def nrm(k, shape, scale):
        return jax.random.normal(k, shape, f32) * scale

    a0 = jax.random.uniform(ks[14], (DEPTH, 2, W_A), f32, minval=0.9, maxval=0.999)
    return {
        "x": nrm(ks[0], (BATCH, SEQ, D_MODEL), 1.0),
        "c": nrm(ks[1], (BATCH, D_MODEL), 1.0),
        "ctx": nrm(ks[2], (BATCH, CTX_LEN, D_MODEL), 1.0),
        "c_ctx": nrm(ks[3], (D_MODEL,), 1.0),
        "norm_g": 1.0 + nrm(ks[4], (DEPTH, D_MODEL), 0.02),
        "w_mod": nrm(ks[5], (DEPTH, D_MODEL, 3 * D_MODEL), 0.5 * D_MODEL ** -0.5),
        "b_mod": nrm(ks[6], (DEPTH, 3 * D_MODEL), 0.01),
        "w_in": nrm(ks[7], (DEPTH, D_MODEL, N_IN), D_MODEL ** -0.5),
        "lru_conv_w": nrm(ks[8], (DEPTH, 2, LRU_CONV, W_A), LRU_CONV ** -0.5),
        "lru_conv_b": nrm(ks[9], (DEPTH, 2, W_A), 0.01),
        "lru_wa": nrm(ks[10], (DEPTH, 2, LRU_HEADS, LRU_HEAD_DIM, LRU_HEAD_DIM), LRU_HEAD_DIM ** -0.5),
        "lru_ba": nrm(ks[11], (DEPTH, 2, W_A), 0.01),
        "lru_wx": nrm(ks[12], (DEPTH, 2, LRU_HEADS, LRU_HEAD_DIM, LRU_HEAD_DIM), LRU_HEAD_DIM ** -0.5),
        "lru_bx": nrm(ks[13], (DEPTH, 2, W_A), 0.01),
        "lru_lambda": jnp.log(a0) - jnp.log1p(-a0),
        "cf_conv_w": nrm(ks[15], (DEPTH, CF_CONV, W_B), CF_CONV ** -0.5),
        "cf_conv_b": nrm(ks[16], (DEPTH, W_B), 0.01),
        "cf_ln_g": 1.0 + nrm(ks[17], (DEPTH, W_B), 0.02),
        "cf_ln_b": nrm(ks[18], (DEPTH, W_B), 0.01),
        "sc_conv_w": nrm(ks[19], (DEPTH, SC_CONV, W_C), SC_CONV ** -0.5),
        "w_branch": nrm(ks[20], (DEPTH, N_BRANCH, W_BR, D_MODEL), W_BR ** -0.5),
        "w_out": nrm(ks[21], (DEPTH, D_MODEL, D_MODEL), D_MODEL ** -0.5),
        "final_g": 1.0 + nrm(ks[22], (D_MODEL,), 0.02),
    }


def reference(x, c, ctx, c_ctx, norm_g, w_mod, b_mod, w_in, lru_conv_w, lru_conv_b, lru_wa, lru_ba,
              lru_wx, lru_bx, lru_lambda, cf_conv_w, cf_conv_b, cf_ln_g, cf_ln_b, sc_conv_w,
              w_branch, w_out, final_g):
    silu_c = jax.nn.silu(c)
    silu_cc = jax.nn.silu(c_ctx)
    xc = ctx
    for l in range(DEPTH):
        last = l == DEPTH - 1
        p = {
            "lru_conv_w": lru_conv_w[l], "lru_conv_b": lru_conv_b[l],
            "lru_wa": lru_wa[l], "lru_ba": lru_ba[l], "lru_wx": lru_wx[l], "lru_bx": lru_bx[l],
            "lru_lambda": lru_lambda[l],
            "cf_conv_w": cf_conv_w[l], "cf_conv_b": cf_conv_b[l],
            "cf_ln_g": cf_ln_g[l], "cf_ln_b": cf_ln_b[l],
            "sc_conv_w": sc_conv_w[l], "w_branch": w_branch[l], "w_out": w_out[l],
        }
        mod = silu_c @ w_mod[l] + b_mod[l]
        shift, scale, gate = jnp.split(mod, 3, axis=-1)
        h = _rmsnorm(x, norm_g[l]) * (1.0 + scale[:, None]) + shift[:, None]
        z = h @ w_in[l]
        n_mod_c = 2 * D_MODEL if last else 3 * D_MODEL
        mod_c = silu_cc @ w_mod[l][:, :n_mod_c] + b_mod[l][:n_mod_c]
        hc = _rmsnorm(xc, norm_g[l]) * (1.0 + mod_c[D_MODEL:2 * D_MODEL]) + mod_c[:D_MODEL]
        n_in_c = W_A if last else N_IN
        zc = hc @ w_in[l][:, :n_in_c]
        hc_f, hc_b = _rglru_mixer(zc[..., :W_A], p, None, None)
        hl_f, hl_b = _rglru_mixer(z[..., :W_A], p, hc_f[:, -1], hc_b[:, 0])
        out = _token_mix(z, (hl_f + hl_b).astype(z.dtype), p)
        if not last:
            out_c = _token_mix(zc, (hc_f + hc_b).astype(zc.dtype), p)
            xc = xc + mod_c[2 * D_MODEL:] * out_c
        x = x + gate[:, None] * out
    return _rmsnorm(x, final_g)
```

```python
import functools
import math

import numpy as np
import jax
import jax.numpy as jnp
from jax import lax
from jax.experimental import pallas as pl
from jax.experimental.pallas import tpu as pltpu

EPS = 1e-6
LRU_C = 8.0
N_BRANCH = 4
FNET_GROUPS = 4
V7X_VMEM_LIMIT_BYTES = 56 * 1024 * 1024
BF16_SUBLANES = 16
F32 = jnp.float32
BF16 = jnp.bfloat16


def _params(*sem):
    return pltpu.CompilerParams(dimension_semantics=sem, vmem_limit_bytes=V7X_VMEM_LIMIT_BYTES)


def _sigmoid(v):
    return jax.nn.sigmoid(v)


def _silu(v):
    return v * jax.nn.sigmoid(v)


def _mod_kernel(c_ref, w_ref, b_ref, o_ref):
    s = _silu(c_ref[...]).astype(BF16)
    w = w_ref[0].astype(BF16)
    o_ref[0] = jnp.dot(s, w, preferred_element_type=F32) + b_ref[0]


def _adaln_mod(cin, w_mod, b_mod):
    depth, d, n = w_mod.shape
    tn = min(512, n)
    return pl.pallas_call(
        _mod_kernel,
        grid=(depth, n // tn),
        in_specs=[pl.BlockSpec((8, d), lambda l, j: (0, 0)),
                  pl.BlockSpec((1, d, tn), lambda l, j: (l, 0, j)),
                  pl.BlockSpec((1, 1, tn), lambda l, j: (l, 0, j))],
        out_specs=pl.BlockSpec((1, 8, tn), lambda l, j: (l, 0, j)),
        out_shape=jax.ShapeDtypeStruct((depth, 8, n), F32),
        compiler_params=_params("parallel", "parallel"),
        name="adaln_mod",
    )(cin, w_mod, b_mod.reshape(depth, 1, n))


def _norm_mod_kernel(x_ref, g_ref, sc_ref, sh_ref, o_ref):
    x = x_ref[...]
    n = x * lax.rsqrt(jnp.mean(x * x, axis=-1, keepdims=True) + EPS) * g_ref[...]
    o_ref[...] = (n * (1.0 + sc_ref[...]) + sh_ref[...]).astype(o_ref.dtype)


def _norm_kernel(x_ref, g_ref, o_ref):
    x = x_ref[...]
    n = x * lax.rsqrt(jnp.mean(x * x, axis=-1, keepdims=True) + EPS) * g_ref[...]
    o_ref[...] = n.astype(o_ref.dtype)


def _row_tile(l, cap):
    t = min(l, cap)
    while l % t:
        t //= 2
    return t


def _norm_mod(x, g, scale, shift):
    b, l, d = x.shape
    tr = _row_tile(l, 256)
    return pl.pallas_call(
        _norm_mod_kernel,
        grid=(b, l // tr),
        in_specs=[pl.BlockSpec((None, tr, d), lambda bi, i: (bi, i, 0)),
                  pl.BlockSpec((1, d), lambda bi, i: (0, 0)),
                  pl.BlockSpec((None, 1, d), lambda bi, i: (bi, 0, 0)),
                  pl.BlockSpec((None, 1, d), lambda bi, i: (bi, 0, 0))],
        out_specs=pl.BlockSpec((None, tr, d), lambda bi, i: (bi, i, 0)),
        out_shape=jax.ShapeDtypeStruct((b, l, d), BF16),
        compiler_params=_params("parallel", "parallel"),
        name="norm_mod",
    )(x, g, scale, shift)


def _final_norm(x, g):
    b, l, d = x.shape
    tr = _row_tile(l, 256)
    return pl.pallas_call(
        _norm_kernel,
        grid=(b, l // tr),
        in_specs=[pl.BlockSpec((None, tr, d), lambda bi, i: (bi, i, 0)),
                  pl.BlockSpec((1, d), lambda bi, i: (0, 0))],
        out_specs=pl.BlockSpec((None, tr, d), lambda bi, i: (bi, i, 0)),
        out_shape=jax.ShapeDtypeStruct((b, l, d), x.dtype),
        compiler_params=_params("parallel", "parallel"),
        name="final_norm",
    )(x, g)


def _matmul_kernel(a_ref, b_ref, o_ref):
    o_ref[...] = jnp.dot(a_ref[...], b_ref[...], preferred_element_type=F32).astype(o_ref.dtype)


def _matmul(a, b, n_cols, out_dtype, name):
    m, k = a.shape
    tm = _row_tile(m, 1024)
    tn = _row_tile(n_cols, 1024)
    return pl.pallas_call(
        _matmul_kernel,
        grid=(m // tm, n_cols // tn),
        in_specs=[pl.BlockSpec((tm, k), lambda i, j: (i, 0)),
                  pl.BlockSpec((k, tn), lambda i, j: (0, j))],
        out_specs=pl.BlockSpec((tm, tn), lambda i, j: (i, j)),
        out_shape=jax.ShapeDtypeStruct((m, n_cols), out_dtype),
        compiler_params=_params("parallel", "parallel"),
        name=name,
    )(a, b)


_LRU_ROW_CONV_B, _LRU_ROW_BA, _LRU_ROW_BX, _LRU_ROW_LAMBDA = 4, 5, 6, 7


def _lru_kernel(*refs, seq, chunk, n_conv, with_out):
    if with_out:
        x_ref, s_ref, wg_ref, prm_ref, h0_ref, y_ref, hfin_ref, hf_ref, hb_ref = refs
    else:
        x_ref, wg_ref, prm_ref, h0_ref, hfin_ref = refs
    cb = x_ref.shape[-1]
    n_chunks = seq // chunk
    groups = chunk // 8
    halo = BF16_SUBLANES
    row = lax.broadcasted_iota(jnp.int32, (groups, 8, cb), 1)

    def gates(u, d):
        prm = prm_ref[d]
        g = jnp.dot(u.astype(BF16), wg_ref[d], preferred_element_type=F32)
        r = _sigmoid(g[:, :cb] + prm[_LRU_ROW_BA:_LRU_ROW_BA + 1])
        i = _sigmoid(g[:, cb:] + prm[_LRU_ROW_BX:_LRU_ROW_BX + 1])
        lam = prm[_LRU_ROW_LAMBDA:_LRU_ROW_LAMBDA + 1]
        log_sig = jnp.minimum(lam, 0.0) - jnp.log1p(jnp.exp(-jnp.abs(lam)))
        log_a = LRU_C * r * log_sig
        a = jnp.exp(log_a)
        th = jnp.tanh(log_a)
        bb = jnp.sqrt(-2.0 * th / (1.0 - th)) * (i * u)
        return a, bb

    def conv(c, d):
        base = pl.multiple_of(c * chunk, chunk)
        cur = x_ref[pl.ds(base, chunk), :].astype(F32)
        prm = prm_ref[d]
        if d == 0:
            pbase = pl.multiple_of(jnp.maximum(base - halo, 0), halo)
            edge = x_ref[pl.ds(pbase, halo), :].astype(F32) * jnp.where(c > 0, 1.0, 0.0)
            win = jnp.concatenate([edge, cur], axis=0)
            off = halo - (n_conv - 1)
        else:
            nbase = pl.multiple_of(jnp.minimum(base + chunk, seq - halo), halo)
            edge = x_ref[pl.ds(nbase, halo), :].astype(F32) * jnp.where(c < n_chunks - 1, 1.0, 0.0)
            win = jnp.concatenate([cur, edge], axis=0)
            off = 0
        u = jnp.broadcast_to(prm[_LRU_ROW_CONV_B:_LRU_ROW_CONV_B + 1], (chunk, cb))
        for k in range(n_conv):
            u = u + win[off + k:off + k + chunk] * prm[k:k + 1]
        return u

    def scan(a, bb, carry, c, d, dst_ref):
        a3 = a.reshape(groups, 8, cb)
        b3 = bb.reshape(groups, 8, cb)
        for s in (1, 2, 4):
            shift = s if d == 0 else 8 - s
            valid = (row >= s) if d == 0 else (row < 8 - s)
            a_sh = jnp.where(valid, pltpu.roll(a3, shift, axis=1), 1.0)
            b_sh = jnp.where(valid, pltpu.roll(b3, shift, axis=1), 0.0)
            b3 = a3 * b_sh + b3
            a3 = a3 * a_sh
        base = pl.multiple_of(c * chunk, chunk)
        order = range(groups) if d == 0 else range(groups - 1, -1, -1)
        for g in order:
            h = a3[g] * carry + b3[g]
            if dst_ref is not None:
                dst_ref[pl.ds(base + g * 8, 8), :] = h
            carry = h[7:8] if d == 0 else h[0:1]
        return carry

    def body(c, carry):
        cf, cbk = carry
        a, bb = gates(conv(c, 0), 0)
        cf = scan(a, bb, cf, c, 0, hf_ref if with_out else None)
        cr = n_chunks - 1 - c
        a, bb = gates(conv(cr, 1), 1)
        cbk = scan(a, bb, cbk, cr, 1, hb_ref if with_out else None)
        return cf, cbk

    h0 = h0_ref[...]
    cf, cbk = lax.fori_loop(0, n_chunks, body, (h0[0:1], h0[1:2]))
    hfin_ref[...] = jnp.concatenate([cf, cbk], axis=0)

    if with_out:
        def finish(c, _):
            base = pl.multiple_of(c * chunk, chunk)
            s = s_ref[pl.ds(base, chunk), :].astype(F32)
            y = hf_ref[pl.ds(base, chunk), :] + hb_ref[pl.ds(base, chunk), :]
            y_ref[pl.ds(base, chunk), :] = (y * _silu(s)).astype(y_ref.dtype)
            return 0
        lax.fori_loop(0, n_chunks, finish, 0)


def _lru_mixer(z, s_col_blk, wg, prm, h0, width, with_out):
    b, l, _ = z.shape
    n_cb, _, cb, _ = wg.shape
    n_conv = 4
    chunk = _row_tile(l, 256)
    kern = functools.partial(_lru_kernel, seq=l, chunk=chunk, n_conv=n_conv, with_out=with_out)
    x_spec = pl.BlockSpec((None, l, cb), lambda bi, ci: (bi, 0, ci))
    s_spec = pl.BlockSpec((None, l, cb), lambda bi, ci: (bi, 0, s_col_blk + ci))
    wg_spec = pl.BlockSpec((None, 2, cb, 2 * cb), lambda bi, ci: (ci, 0, 0, 0))
    prm_spec = pl.BlockSpec((2, 8, cb), lambda bi, ci: (0, 0, ci))
    st_spec = pl.BlockSpec((None, 2, cb), lambda bi, ci: (bi, 0, ci))
    st_shape = jax.ShapeDtypeStruct((b, 2, width), F32)
    if with_out:
        y, hfin = pl.pallas_call(
            kern, grid=(b, n_cb),
            in_specs=[x_spec, s_spec, wg_spec, prm_spec, st_spec],
            out_specs=[pl.BlockSpec((None, l, cb), lambda bi, ci: (bi, 0, ci)), st_spec],
            out_shape=[jax.ShapeDtypeStruct((b, l, width), BF16), st_shape],
            scratch_shapes=[pltpu.VMEM((l, cb), F32), pltpu.VMEM((l, cb), F32)],
            compiler_params=_params("parallel", "parallel"),
            name="rglru_mixer",
        )(z, z, wg, prm, h0)
        return y, hfin
    hfin = pl.pallas_call(
        kern, grid=(b, n_cb),
        in_specs=[x_spec, wg_spec, prm_spec, st_spec],
        out_specs=st_spec,
        out_shape=st_shape,
        compiler_params=_params("parallel", "parallel"),
        name="rglru_state",
    )(z, wg, prm, h0)
    return None, hfin


_CONV_ROWS = 64
_CONV_SLAB = 96
_EPI_ROWS = 32
_LANES = 128


def _halo_specs(t, width, col_blk, n_row_blocks):
    r = t // BF16_SUBLANES
    last = n_row_blocks * r - 1
    cur = pl.BlockSpec((None, t, width), lambda bi, i: (bi, i, col_blk))
    prev = pl.BlockSpec((None, BF16_SUBLANES, width), lambda bi, i: (bi, jnp.maximum(i * r - 1, 0), col_blk))
    nxt = pl.BlockSpec((None, BF16_SUBLANES, width), lambda bi, i: (bi, jnp.minimum((i + 1) * r, last), col_blk))
    return prev, cur, nxt


def _fill_window(win_ref, fn, prev_refs, cur_refs, next_refs, t):
    i = pl.program_id(1)
    n = pl.num_programs(1)
    h = BF16_SUBLANES
    load = lambda refs: [r[...].astype(F32) for r in refs]
    win_ref[0:h] = fn(*load(prev_refs)) * jnp.where(i > 0, 1.0, 0.0)
    win_ref[h:h + t] = fn(*load(cur_refs))
    win_ref[h + t:h + t + h] = fn(*load(next_refs)) * jnp.where(i < n - 1, 1.0, 0.0)


def _dw_conv(win_ref, wb_ref, dst_ref, t, n_taps):
    off = BF16_SUBLANES - n_taps // 2
    assert off >= 0 and off + n_taps - 1 + _CONV_ROWS <= _CONV_SLAB
    width = dst_ref.shape[-1]
    groups = _CONV_ROWS // 8

    def tile(ti, _):
        r0 = pl.multiple_of(ti * _CONV_ROWS, _CONV_ROWS)
        for c0 in range(0, width, _LANES):
            slab = win_ref[pl.ds(r0, _CONV_SLAB), c0:c0 + _LANES]
            acc = None
            for res in range(8):
                taps = [o for o in range(off, off + n_taps) if o % 8 == res]
                if not taps:
                    continue
                sh = slab if res == 0 else pltpu.roll(slab, _CONV_SLAB - res, axis=0)
                for o in taps:
                    q = 8 * (o // 8)
                    term = sh[q:q + _CONV_ROWS].reshape(groups, 8, _LANES) * wb_ref[o - off][:, c0:c0 + _LANES]
                    acc = term if acc is None else acc + term
            dst_ref[pl.ds(r0, _CONV_ROWS), c0:c0 + _LANES] = acc.reshape(_CONV_ROWS, _LANES)
        return 0

    lax.fori_loop(0, t // _CONV_ROWS, tile, 0)


def _conformer_kernel(vp, gp, vc, gc, vn, gn, s_ref, wb_ref, prm_ref, o_ref, win_ref, conv_ref, *, t, n_taps):
    _fill_window(win_ref, lambda v, g: v * _sigmoid(g), (vp, gp), (vc, gc), (vn, gn), t)
    _dw_conv(win_ref, wb_ref, conv_ref, t, n_taps)

    def tile(ti, _):
        r0 = pl.multiple_of(ti * _EPI_ROWS, _EPI_ROWS)
        acc = conv_ref[pl.ds(r0, _EPI_ROWS), :] + prm_ref[0:1, :]
        mu = jnp.mean(acc, axis=-1, keepdims=True)
        d = acc - mu
        var = jnp.mean(d * d, axis=-1, keepdims=True)
        yn = d * lax.rsqrt(var + EPS) * prm_ref[1:2, :] + prm_ref[2:3, :]
        s = s_ref[pl.ds(r0, _EPI_ROWS), :].astype(F32)
        o_ref[pl.ds(r0, _EPI_ROWS), :] = (_silu(yn) * _silu(s)).astype(o_ref.dtype)
        return 0

    lax.fori_loop(0, t // _EPI_ROWS, tile, 0)


def _shortconv_kernel(xp, cp, xc, cc, xn, cn, gb_ref, s_ref, wb_ref, o_ref, win_ref, conv_ref, *, t, n_taps):
    _fill_window(win_ref, lambda x, c: c * x, (xp, cp), (xc, cc), (xn, cn), t)
    _dw_conv(win_ref, wb_ref, conv_ref, t, n_taps)

    def tile(ti, _):
        r0 = pl.multiple_of(ti * _EPI_ROWS, _EPI_ROWS)
        gb = gb_ref[pl.ds(r0, _EPI_ROWS), :].astype(F32)
        s = s_ref[pl.ds(r0, _EPI_ROWS), :].astype(F32)
        o_ref[pl.ds(r0, _EPI_ROWS), :] = (gb * conv_ref[pl.ds(r0, _EPI_ROWS), :] * _silu(s)).astype(o_ref.dtype)
        return 0

    lax.fori_loop(0, t // _EPI_ROWS, tile, 0)


def _sublane_repeat(w):
    return jnp.broadcast_to(w.astype(F32)[:, None, :], (w.shape[0], 8, w.shape[1]))


def _conv_scratch(t, width):
    return [pltpu.VMEM((t + 2 * BF16_SUBLANES, width), F32), pltpu.VMEM((t, width), F32)]


def _conformer_mixer(z, w_conv, prm, width, col_val, col_gl, col_s):
    b, l, _ = z.shape
    t = _row_tile(l, 512)
    nb = l // t
    n_taps = w_conv.shape[0]
    wb = _sublane_repeat(w_conv)
    vp, vc, vn = _halo_specs(t, width, col_val, nb)
    gp, gc, gn = _halo_specs(t, width, col_gl, nb)
    return pl.pallas_call(
        functools.partial(_conformer_kernel, t=t, n_taps=n_taps),
        grid=(b, nb),
        in_specs=[vp, gp, vc, gc, vn, gn,
                  pl.BlockSpec((None, t, width), lambda bi, i: (bi, i, col_s)),
                  pl.BlockSpec(wb.shape, lambda bi, i: (0, 0, 0)),
                  pl.BlockSpec(prm.shape, lambda bi, i: (0, 0))],
        out_specs=pl.BlockSpec((None, t, width), lambda bi, i: (bi, i, 0)),
        out_shape=jax.ShapeDtypeStruct((b, l, width), BF16),
        scratch_shapes=_conv_scratch(t, width),
        compiler_params=_params("parallel", "parallel"),
        name="conformer_conv",
    )(z, z, z, z, z, z, z, wb, prm)


def _shortconv_mixer(z, w_conv, width, col_x, col_gb, col_gc, col_s):
    b, l, _ = z.shape
    t = _row_tile(l, 512)
    nb = l // t
    n_taps = w_conv.shape[0]
    wb = _sublane_repeat(w_conv)
    xp, xc, xn = _halo_specs(t, width, col_x, nb)
    cp, cc, cn = _halo_specs(t, width, col_gc, nb)
    return pl.pallas_call(
        functools.partial(_shortconv_kernel, t=t, n_taps=n_taps),
        grid=(b, nb),
        in_specs=[xp, cp, xc, cc, xn, cn,
                  pl.BlockSpec((None, t, width), lambda bi, i: (bi, i, col_gb)),
                  pl.BlockSpec((None, t, width), lambda bi, i: (bi, i, col_s)),
                  pl.BlockSpec(wb.shape, lambda bi, i: (0, 0, 0))],
        out_specs=pl.BlockSpec((None, t, width), lambda bi, i: (bi, i, 0)),
        out_shape=jax.ShapeDtypeStruct((b, l, width), BF16),
        scratch_shapes=_conv_scratch(t, width),
        compiler_params=_params("parallel", "parallel"),
        name="short_conv",
    )(z, z, z, z, z, z, z, z, wb)


def _dft_split(n):
    n2 = 1 << (int(math.log2(n)) // 2)
    n1 = n // n2
    assert n1 * n2 == n
    return n1, n2


def _cos_sin(rows, cols, period):
    ang = 2.0 * np.pi * ((np.outer(np.arange(rows), np.arange(cols))) % period) / period
    return np.cos(ang), np.sin(ang)


def _fnet_tables(seq, group_width):
    n1, n2 = _dft_split(seq)
    cc, sc = _cos_sin(group_width, group_width, group_width)
    c1, s1 = _cos_sin(n1, n1, n1)
    c2, s2 = _cos_sin(n2, n2, n2)
    tc, ts = _cos_sin(n2, n1, seq)
    lanes = 128
    tables = dict(
        cs=np.concatenate([cc, sc], axis=1),
        m1=np.block([[c1, -s1], [s1, c1]]),
        m2=np.concatenate([c2, -s2], axis=1),
        tc=np.broadcast_to(tc[:, :, None], (n2, n1, lanes)),
        ts=np.broadcast_to(ts[:, :, None], (n2, n1, lanes)),
    )
    out = {k: jnp.asarray(np.ascontiguousarray(v), F32) for k, v in tables.items()}
    for k in ("cs", "m1", "m2"):
        out[k] = out[k].astype(BF16)
    return out


def _fnet_stage1_kernel(x_ref, cs_ref, m1_ref, tc_ref, ts_ref, or_ref, oi_ref, *, groups):
    nb, n1, width = x_ref.shape
    gw = width // groups
    x = x_ref[...].reshape(nb * n1, width)
    ps, qs = [], []
    for g in range(groups):
        pq = jnp.dot(x[:, g * gw:(g + 1) * gw], cs_ref[...], preferred_element_type=F32)
        ps.append(pq[:, :gw])
        qs.append(pq[:, gw:])
    p = jnp.concatenate(ps, axis=1).astype(BF16)
    q = jnp.concatenate(qs, axis=1).astype(BF16)
    reps = width // tc_ref.shape[-1]
    for j in range(nb):
        pq = jnp.concatenate([p[j * n1:(j + 1) * n1], q[j * n1:(j + 1) * n1]], axis=0)
        a = jnp.dot(m1_ref[...], pq, preferred_element_type=F32)
        ar, ai = a[:n1], a[n1:]
        tc = jnp.tile(tc_ref[j], (1, reps))
        ts = jnp.tile(ts_ref[j], (1, reps))
        or_ref[j] = (ar * tc - ai * ts).astype(or_ref.dtype)
        oi_ref[j] = (ar * ts + ai * tc).astype(oi_ref.dtype)


def _fnet_stage2_kernel(ar_ref, ai_ref, m2_ref, o_ref, *, scale):
    kb = ar_ref.shape[0]
    for j in range(kb):
        rhs = jnp.concatenate([ar_ref[j], ai_ref[j]], axis=0)
        o_ref[j] = (jnp.dot(m2_ref[...], rhs, preferred_element_type=F32) * scale).astype(o_ref.dtype)


def _gate_kernel(y_ref, s_ref, o_ref):
    s = s_ref[...].astype(F32)
    o_ref[...] = (y_ref[...].astype(F32) * _silu(s)).astype(o_ref.dtype)


def _fnet_mixer(zd, z, col_s, tables):
    b, l, width = zd.shape
    n1, n2 = _dft_split(l)
    gw = width // FNET_GROUPS
    nb = max(1, min(n2, 1024 // n1))
    xt = zd.reshape(b, n1, n2, width).transpose(0, 2, 1, 3)
    blk = pl.BlockSpec((None, nb, n1, width), lambda bi, j: (bi, j, 0, 0))
    tw = pl.BlockSpec((nb, n1, 128), lambda bi, j: (j, 0, 0))
    full = lambda a: pl.BlockSpec(a.shape, lambda bi, j: (0,) * a.ndim)
    ar, ai = pl.pallas_call(
        functools.partial(_fnet_stage1_kernel, groups=FNET_GROUPS),
        grid=(b, n2 // nb),
        in_specs=[blk, full(tables["cs"]), full(tables["m1"]), tw, tw],
        out_specs=[blk, blk],
        out_shape=[jax.ShapeDtypeStruct((b, n2, n1, width), BF16)] * 2,
        compiler_params=_params("parallel", "parallel"),
        name="fnet_stage1",
    )(xt, tables["cs"], tables["m1"], tables["tc"], tables["ts"])
    ar = ar.transpose(0, 2, 1, 3)
    ai = ai.transpose(0, 2, 1, 3)
    kb = min(n1, 8)
    blk2 = pl.BlockSpec((None, kb, n2, width), lambda bi, j: (bi, j, 0, 0))
    scale = 1.0 / math.sqrt(l * gw)
    f = pl.pallas_call(
        functools.partial(_fnet_stage2_kernel, scale=scale),
        grid=(b, n1 // kb),
        in_specs=[blk2, blk2, full(tables["m2"])],
        out_specs=blk2,
        out_shape=jax.ShapeDtypeStruct((b, n1, n2, width), BF16),
        compiler_params=_params("parallel", "parallel"),
        name="fnet_stage2",
    )(ar, ai, tables["m2"])
    f = f.transpose(0, 2, 1, 3).reshape(b, l, width)
    t = _row_tile(l, 512)
    return pl.pallas_call(
        _gate_kernel,
        grid=(b, l // t),
        in_specs=[pl.BlockSpec((None, t, width), lambda bi, i: (bi, i, 0)),
                  pl.BlockSpec((None, t, width), lambda bi, i: (bi, i, col_s))],
        out_specs=pl.BlockSpec((None, t, width), lambda bi, i: (bi, i, 0)),
        out_shape=jax.ShapeDtypeStruct((b, l, width), BF16),
        compiler_params=_params("parallel", "parallel"),
        name="fnet_gate",
    )(f, z)


def _branch_kernel(pa, pb, pc, pd, w_ref, ga, gb, gc, gd, o_ref):
    acc = None
    for k, (p, g) in enumerate(((pa, ga), (pb, gb), (pc, gc), (pd, gd))):
        br = jnp.dot(p[...], w_ref[k], preferred_element_type=F32)
        term = _sigmoid(g[...].astype(F32)) * br
        acc = term if acc is None else acc + term
    o_ref[...] = acc.astype(o_ref.dtype)


def _branch_merge(ps, w_branch, z2d, gate_col0, d_model):
    m, width = ps[0].shape
    tm = _row_tile(m, 1024)
    tn = _row_tile(d_model, 512)
    g0 = gate_col0 // tn
    per = d_model // tn
    p_spec = pl.BlockSpec((tm, width), lambda i, j: (i, 0))
    g_specs = [pl.BlockSpec((tm, tn), functools.partial(lambda i, j, k: (i, g0 + k * per + j), k=k))
               for k in range(N_BRANCH)]
    return pl.pallas_call(
        _branch_kernel,
        grid=(m // tm, d_model // tn),
        in_specs=[p_spec] * 4 + [pl.BlockSpec((N_BRANCH, width, tn), lambda i, j: (0, 0, j))] + g_specs,
        out_specs=pl.BlockSpec((tm, tn), lambda i, j: (i, j)),
        out_shape=jax.ShapeDtypeStruct((m, d_model), BF16),
        compiler_params=_params("parallel", "parallel"),
        name="branch_merge",
    )(*ps, w_branch, z2d, z2d, z2d, z2d)


def _out_kernel(m_ref, w_ref, x_ref, gate_ref, o_ref):
    out = jnp.dot(m_ref[...], w_ref[...], preferred_element_type=F32)
    o_ref[...] = x_ref[...] + gate_ref[...] * out


def _out_proj_residual(merged, w_out, x, gate):
    b, l, d = x.shape
    tm = _row_tile(l, 1024)
    tn = _row_tile(d, 512)
    return pl.pallas_call(
        _out_kernel,
        grid=(b, l // tm, d // tn),
        in_specs=[pl.BlockSpec((None, tm, d), lambda bi, i, j: (bi, i, 0)),
                  pl.BlockSpec((d, tn), lambda bi, i, j: (0, j)),
                  pl.BlockSpec((None, tm, tn), lambda bi, i, j: (bi, i, j)),
                  pl.BlockSpec((None, 1, tn), lambda bi, i, j: (bi, 0, j))],
        out_specs=pl.BlockSpec((None, tm, tn), lambda bi, i, j: (bi, i, j)),
        out_shape=jax.ShapeDtypeStruct((b, l, d), x.dtype),
        input_output_aliases={2: 0},
        compiler_params=_params("parallel", "parallel", "parallel"),
        name="out_proj_residual",
    )(merged, w_out, x, gate)


def _block_diag_gates(wa, wx, cb):
    depth, _, heads, hd, _ = wa.shape
    hpb = cb // hd
    n_cb = heads // hpb
    eye = jnp.eye(hpb, dtype=wa.dtype)

    def bd(w):
        w = w.reshape(depth, 2, n_cb, hpb, hd, hd)
        full = jnp.einsum("ldchij,hg->ldchigj", w, eye)
        return full.reshape(depth, 2, n_cb, cb, cb)

    both = jnp.concatenate([bd(wa), bd(wx)], axis=-1)
    return both.transpose(0, 2, 1, 3, 4).astype(BF16)


def _token_mix(z, zd, y_a, width, d_model, lp, tables):
    b, l, n_main = z.shape
    y_b = _conformer_mixer(z, lp["cf_conv_w"], lp["cf_prm"], width, 1, 2, 7)
    y_c = _shortconv_mixer(z, lp["sc_conv_w"], width, 3, 4, 5, 8)
    y_d = _fnet_mixer(zd, z, 9, tables)
    m = b * l
    ps = [y.reshape(m, width) for y in (y_a, y_b, y_c, y_d)]
    merged = _branch_merge(ps, lp["w_branch"], z.reshape(m, n_main), 10 * width, d_model)
    return merged.reshape(b, l, d_model)


def kernel(x, c, ctx, c_ctx, norm_g, w_mod, b_mod, w_in, lru_conv_w, lru_conv_b, lru_wa, lru_ba, lru_wx,
           lru_bx, lru_lambda, cf_conv_w, cf_conv_b, cf_ln_g, cf_ln_b, sc_conv_w, w_branch, w_out, final_g):
    bsz, seq, d_model = x.shape
    ctx_len = ctx.shape[1]
    depth = w_in.shape[0]
    width = d_model // N_BRANCH
    assert bsz + 1 <= 8

    d_lo, d_hi = 6 * width, 7 * width
    w_main = jnp.concatenate([w_in[:, :, :d_lo], w_in[:, :, d_hi:]], axis=-1).astype(BF16)
    w_four = w_in[:, :, d_lo:d_hi].astype(BF16)
    n_main = w_main.shape[-1]
    w_branch_b = w_branch.astype(BF16)
    w_out_b = w_out.astype(BF16)

    cb = min(256, width)
    wg = _block_diag_gates(lru_wa, lru_wx, cb)
    lru_prm = jnp.concatenate([lru_conv_w, lru_conv_b[:, :, None], lru_ba[:, :, None], lru_bx[:, :, None],
                               lru_lambda[:, :, None]], axis=2).astype(F32)
    cf_prm = jnp.pad(jnp.stack([cf_conv_b, cf_ln_g, cf_ln_b], axis=1), ((0, 0), (0, 5), (0, 0)))
    tables_lat = _fnet_tables(seq, width // FNET_GROUPS)
    tables_ctx = _fnet_tables(ctx_len, width // FNET_GROUPS)

    cin = jnp.concatenate([c, c_ctx[None], jnp.zeros((8 - bsz - 1, d_model), c.dtype)], axis=0)
    mods = _adaln_mod(cin, w_mod, b_mod)

    xc = ctx
    s_blk = (6 * width) // cb
    for l in range(depth):
        last = l == depth - 1
        lp = dict(cf_conv_w=cf_conv_w[l], cf_prm=cf_prm[l], sc_conv_w=sc_conv_w[l], w_branch=w_branch_b[l])
        g = norm_g[l][None]
        shift, scale, gate = (mods[l, :, k * d_model:(k + 1) * d_model] for k in range(3))
        rows_c = jnp.full((bsz,), bsz, jnp.int32)

        hc = _norm_mod(xc, g, scale[rows_c][:, None], shift[rows_c][:, None])
        hc2 = hc.reshape(bsz * ctx_len, d_model)
        zc = _matmul(hc2, w_main[l], width if last else n_main, BF16, "in_proj_ctx")
        zc = zc.reshape(bsz, ctx_len, -1)
        h0 = jnp.zeros((bsz, 2, width), F32)
        yc_a, h_ctx = _lru_mixer(zc, s_blk, wg[l], lru_prm[l], h0, width, with_out=not last)

        h = _norm_mod(x, g, scale[:bsz, None], shift[:bsz, None])
        h2 = h.reshape(bsz * seq, d_model)
        z = _matmul(h2, w_main[l], n_main, BF16, "in_proj").reshape(bsz, seq, n_main)
        zd = _matmul(h2, w_four[l], width, BF16, "in_proj_fourier").reshape(bsz, seq, width)
        y_a, _ = _lru_mixer(z, s_blk, wg[l], lru_prm[l], h_ctx, width, with_out=True)
        merged = _token_mix(z, zd, y_a, width, d_model, lp, tables_lat)
        if not last:
            zcd = _matmul(hc2, w_four[l], width, BF16, "in_proj_fourier_ctx").reshape(bsz, ctx_len, width)
            merged_c = _token_mix(zc, zcd, yc_a, width, d_model, lp, tables_ctx)
            xc = _out_proj_residual(merged_c, w_out_b[l], xc, gate[rows_c][:, None])
        x = _out_proj_residual(merged, w_out_b[l], x, gate[:bsz, None])
    return _final_norm(x, final_g[None])
```

```python
import functools
import math

import numpy as np
import jax
import jax.numpy as jnp
from jax import lax
from jax.experimental import pallas as pl
from jax.experimental.pallas import tpu as pltpu

EPS = 1e-6
LRU_C = 8.0
N_BRANCH = 4
FNET_GROUPS = 4
V7X_VMEM_LIMIT_BYTES = 56 * 1024 * 1024
BF16_SUBLANES = 16
F32 = jnp.float32
BF16 = jnp.bfloat16


def _params(*sem):
    return pltpu.CompilerParams(dimension_semantics=sem, vmem_limit_bytes=V7X_VMEM_LIMIT_BYTES)


def _sigmoid(v):
    return 0.5 * jnp.tanh(0.5 * v) + 0.5


def _silu(v):
    return v * _sigmoid(v)


def _mod_kernel(c_ref, w_ref, b_ref, o_ref):
    s = _silu(c_ref[...]).astype(BF16)
    w = w_ref[0].astype(BF16)
    o_ref[0] = jnp.dot(s, w, preferred_element_type=F32) + b_ref[0]


def _adaln_mod(cin, w_mod, b_mod):
    depth, d, n = w_mod.shape
    tn = min(512, n)
    return pl.pallas_call(
        _mod_kernel,
        grid=(depth, n // tn),
        in_specs=[pl.BlockSpec((8, d), lambda l, j: (0, 0)),
                  pl.BlockSpec((1, d, tn), lambda l, j: (l, 0, j)),
                  pl.BlockSpec((1, 1, tn), lambda l, j: (l, 0, j))],
        out_specs=pl.BlockSpec((1, 8, tn), lambda l, j: (l, 0, j)),
        out_shape=jax.ShapeDtypeStruct((depth, 8, n), F32),
        compiler_params=_params("parallel", "parallel"),
        name="adaln_mod",
    )(cin, w_mod, b_mod.reshape(depth, 1, n))


def _norm_mod_kernel(x_ref, g_ref, sc_ref, sh_ref, o_ref):
    x = x_ref[...]
    n = x * lax.rsqrt(jnp.mean(x * x, axis=-1, keepdims=True) + EPS) * g_ref[...]
    o_ref[...] = (n * (1.0 + sc_ref[...]) + sh_ref[...]).astype(o_ref.dtype)


def _norm_kernel(x_ref, g_ref, o_ref):
    x = x_ref[...]
    n = x * lax.rsqrt(jnp.mean(x * x, axis=-1, keepdims=True) + EPS) * g_ref[...]
    o_ref[...] = n.astype(o_ref.dtype)


def _row_tile(l, cap):
    t = min(l, cap)
    while l % t:
        t //= 2
    return t


def _norm_mod(x, g, scale, shift):
    b, l, d = x.shape
    tr = _row_tile(l, 256)
    return pl.pallas_call(
        _norm_mod_kernel,
        grid=(b, l // tr),
        in_specs=[pl.BlockSpec((None, tr, d), lambda bi, i: (bi, i, 0)),
                  pl.BlockSpec((1, d), lambda bi, i: (0, 0)),
                  pl.BlockSpec((None, 1, d), lambda bi, i: (bi, 0, 0)),
                  pl.BlockSpec((None, 1, d), lambda bi, i: (bi, 0, 0))],
        out_specs=pl.BlockSpec((None, tr, d), lambda bi, i: (bi, i, 0)),
        out_shape=jax.ShapeDtypeStruct((b, l, d), BF16),
        compiler_params=_params("parallel", "parallel"),
        name="norm_mod",
    )(x, g, scale, shift)


def _final_norm(x, g):
    b, l, d = x.shape
    tr = _row_tile(l, 256)
    return pl.pallas_call(
        _norm_kernel,
        grid=(b, l // tr),
        in_specs=[pl.BlockSpec((None, tr, d), lambda bi, i: (bi, i, 0)),
                  pl.BlockSpec((1, d), lambda bi, i: (0, 0))],
        out_specs=pl.BlockSpec((None, tr, d), lambda bi, i: (bi, i, 0)),
        out_shape=jax.ShapeDtypeStruct((b, l, d), x.dtype),
        compiler_params=_params("parallel", "parallel"),
        name="final_norm",
    )(x, g)


def _in_proj_kernel(h_ref, w_hbm, o_ref, wf_ref, wb_ref, sem, *, layer, first_blk, skip_blk):
    j = pl.program_id(0)
    i = pl.program_id(1)
    tn = wb_ref.shape[-1]

    def panel_copy(jj):
        blk = first_blk + jj
        if skip_blk is not None:
            blk = blk + jnp.where(blk >= skip_blk, 1, 0)
        col = pl.multiple_of(blk * tn, tn)
        return pltpu.make_async_copy(w_hbm.at[layer, :, pl.ds(col, tn)], wf_ref, sem)

    @pl.when(i == 0)
    def _():
        @pl.when(j == 0)
        def _():
            panel_copy(j).start()

        panel_copy(j).wait()
        wb_ref[...] = wf_ref[...].astype(BF16)

        @pl.when(j + 1 < pl.num_programs(0))
        def _():
            panel_copy(j + 1).start()

    o_ref[...] = jnp.dot(h_ref[...], wb_ref[...], preferred_element_type=F32).astype(o_ref.dtype)


def _in_proj(h, w_in, layer, first_blk, n_blk, skip_blk, tn, name):
    m, k = h.shape
    tm = _row_tile(m, 1024)
    return pl.pallas_call(
        functools.partial(_in_proj_kernel, layer=layer, first_blk=first_blk, skip_blk=skip_blk),
        grid=(n_blk, m // tm),
        in_specs=[pl.BlockSpec((tm, k), lambda j, i: (i, 0)),
                  pl.BlockSpec(memory_space=pl.ANY)],
        out_specs=pl.BlockSpec((tm, tn), lambda j, i: (i, j)),
        out_shape=jax.ShapeDtypeStruct((m, n_blk * tn), BF16),
        scratch_shapes=[pltpu.VMEM((k, tn), F32), pltpu.VMEM((k, tn), BF16), pltpu.SemaphoreType.DMA(())],
        compiler_params=_params("arbitrary", "arbitrary"),
        name=name,
    )(h, w_in)


_LRU_ROW_CONV_B, _LRU_ROW_BA, _LRU_ROW_BX, _LRU_ROW_LAMBDA = 4, 5, 6, 7


def _lru_kernel(*refs, seq, chunk, n_conv, with_out):
    if with_out:
        x_ref, s_ref, wg_ref, prm_ref, h0_ref, y_ref, hfin_ref, hf_ref, hb_ref = refs
    else:
        x_ref, wg_ref, prm_ref, h0_ref, hfin_ref = refs
    cb = x_ref.shape[-1]
    n_chunks = seq // chunk
    groups = chunk // 8
    halo = BF16_SUBLANES
    row = lax.broadcasted_iota(jnp.int32, (groups, 8, cb), 1)

    def gates(u, d):
        prm = prm_ref[d]
        g = jnp.dot(u.astype(BF16), wg_ref[d], preferred_element_type=F32)
        r = _sigmoid(g[:, :cb] + prm[_LRU_ROW_BA:_LRU_ROW_BA + 1])
        i = _sigmoid(g[:, cb:] + prm[_LRU_ROW_BX:_LRU_ROW_BX + 1])
        lam = prm[_LRU_ROW_LAMBDA:_LRU_ROW_LAMBDA + 1]
        log_sig = jnp.minimum(lam, 0.0) - jnp.log1p(jnp.exp(-jnp.abs(lam)))
        log_a = LRU_C * r * log_sig
        a = jnp.exp(log_a)
        th = jnp.tanh(log_a)
        bb = jnp.sqrt(-2.0 * th / (1.0 - th)) * (i * u)
        return a, bb

    def conv(c, d):
        base = pl.multiple_of(c * chunk, chunk)
        cur = x_ref[pl.ds(base, chunk), :].astype(F32)
        prm = prm_ref[d]
        if d == 0:
            pbase = pl.multiple_of(jnp.maximum(base - halo, 0), halo)
            edge = x_ref[pl.ds(pbase, halo), :].astype(F32) * jnp.where(c > 0, 1.0, 0.0)
            win = jnp.concatenate([edge, cur], axis=0)
            off = halo - (n_conv - 1)
        else:
            nbase = pl.multiple_of(jnp.minimum(base + chunk, seq - halo), halo)
            edge = x_ref[pl.ds(nbase, halo), :].astype(F32) * jnp.where(c < n_chunks - 1, 1.0, 0.0)
            win = jnp.concatenate([cur, edge], axis=0)
            off = 0
        u = jnp.broadcast_to(prm[_LRU_ROW_CONV_B:_LRU_ROW_CONV_B + 1], (chunk, cb))
        for k in range(n_conv):
            u = u + win[off + k:off + k + chunk] * prm[k:k + 1]
        return u

    def scan(a, bb, carry, c, d, dst_ref):
        a3 = a.reshape(groups, 8, cb)
        b3 = bb.reshape(groups, 8, cb)
        for s in (1, 2, 4):
            shift = s if d == 0 else 8 - s
            valid = (row >= s) if d == 0 else (row < 8 - s)
            a_sh = jnp.where(valid, pltpu.roll(a3, shift, axis=1), 1.0)
            b_sh = jnp.where(valid, pltpu.roll(b3, shift, axis=1), 0.0)
            b3 = a3 * b_sh + b3
            a3 = a3 * a_sh
        base = pl.multiple_of(c * chunk, chunk)
        order = range(groups) if d == 0 else range(groups - 1, -1, -1)
        for g in order:
            h = a3[g] * carry + b3[g]
            if dst_ref is not None:
                dst_ref[pl.ds(base + g * 8, 8), :] = h
            carry = h[7:8] if d == 0 else h[0:1]
        return carry

    def body(c, carry):
        cf, cbk = carry
        a, bb = gates(conv(c, 0), 0)
        cf = scan(a, bb, cf, c, 0, hf_ref if with_out else None)
        cr = n_chunks - 1 - c
        a, bb = gates(conv(cr, 1), 1)
        cbk = scan(a, bb, cbk, cr, 1, hb_ref if with_out else None)
        return cf, cbk

    h0 = h0_ref[...]
    cf, cbk = lax.fori_loop(0, n_chunks, body, (h0[0:1], h0[1:2]))
    hfin_ref[...] = jnp.concatenate([cf, cbk], axis=0)

    if with_out:
        def finish(c, _):
            base = pl.multiple_of(c * chunk, chunk)
            s = s_ref[pl.ds(base, chunk), :].astype(F32)
            y = hf_ref[pl.ds(base, chunk), :] + hb_ref[pl.ds(base, chunk), :]
            y_ref[pl.ds(base, chunk), :] = (y * _silu(s)).astype(y_ref.dtype)
            return 0
        lax.fori_loop(0, n_chunks, finish, 0)


def _lru_mixer(z, s_col_blk, wg, prm, h0, width, with_out):
    b, l, _ = z.shape
    n_cb, _, cb, _ = wg.shape
    n_conv = 4
    chunk = _row_tile(l, 256)
    kern = functools.partial(_lru_kernel, seq=l, chunk=chunk, n_conv=n_conv, with_out=with_out)
    x_spec = pl.BlockSpec((None, l, cb), lambda bi, ci: (bi, 0, ci))
    s_spec = pl.BlockSpec((None, l, cb), lambda bi, ci: (bi, 0, s_col_blk + ci))
    wg_spec = pl.BlockSpec((None, 2, cb, 2 * cb), lambda bi, ci: (ci, 0, 0, 0))
    prm_spec = pl.BlockSpec((2, 8, cb), lambda bi, ci: (0, 0, ci))
    st_spec = pl.BlockSpec((None, 2, cb), lambda bi, ci: (bi, 0, ci))
    st_shape = jax.ShapeDtypeStruct((b, 2, width), F32)
    if with_out:
        y, hfin = pl.pallas_call(
            kern, grid=(b, n_cb),
            in_specs=[x_spec, s_spec, wg_spec, prm_spec, st_spec],
            out_specs=[pl.BlockSpec((None, l, cb), lambda bi, ci: (bi, 0, ci)), st_spec],
            out_shape=[jax.ShapeDtypeStruct((b, l, width), BF16), st_shape],
            scratch_shapes=[pltpu.VMEM((l, cb), F32), pltpu.VMEM((l, cb), F32)],
            compiler_params=_params("parallel", "parallel"),
            name="rglru_mixer",
        )(z, z, wg, prm, h0)
        return y, hfin
    hfin = pl.pallas_call(
        kern, grid=(b, n_cb),
        in_specs=[x_spec, wg_spec, prm_spec, st_spec],
        out_specs=st_spec,
        out_shape=st_shape,
        compiler_params=_params("parallel", "parallel"),
        name="rglru_state",
    )(z, wg, prm, h0)
    return None, hfin


_CONV_ROWS = 64
_CONV_SLAB = 96
_EPI_ROWS = 64
_LANES = 128


def _halo_specs(t, width, col_blk, n_row_blocks):
    r = t // BF16_SUBLANES
    last = n_row_blocks * r - 1
    cur = pl.BlockSpec((None, t, width), lambda bi, i: (bi, i, col_blk))
    prev = pl.BlockSpec((None, BF16_SUBLANES, width), lambda bi, i: (bi, jnp.maximum(i * r - 1, 0), col_blk))
    nxt = pl.BlockSpec((None, BF16_SUBLANES, width), lambda bi, i: (bi, jnp.minimum((i + 1) * r, last), col_blk))
    return prev, cur, nxt


def _fill_window(win_ref, fn, prev_refs, cur_refs, next_refs, t):
    i = pl.program_id(1)
    n = pl.num_programs(1)
    h = BF16_SUBLANES
    load = lambda refs: [r[...].astype(F32) for r in refs]
    win_ref[0:h] = fn(*load(prev_refs)) * jnp.where(i > 0, 1.0, 0.0)
    win_ref[h:h + t] = fn(*load(cur_refs))
    win_ref[h + t:h + t + h] = fn(*load(next_refs)) * jnp.where(i < n - 1, 1.0, 0.0)


def _dw_conv(win_ref, wb_ref, dst_ref, t, n_taps):
    off = BF16_SUBLANES - n_taps // 2
    assert off >= 0 and off + n_taps - 1 + _CONV_ROWS <= _CONV_SLAB
    width = dst_ref.shape[-1]
    groups = _CONV_ROWS // 8

    def tile(ti, _):
        r0 = pl.multiple_of(ti * _CONV_ROWS, _CONV_ROWS)
        for c0 in range(0, width, _LANES):
            slab = win_ref[pl.ds(r0, _CONV_SLAB), c0:c0 + _LANES]
            acc = None
            for res in range(8):
                taps = [o for o in range(off, off + n_taps) if o % 8 == res]
                if not taps:
                    continue
                sh = slab if res == 0 else pltpu.roll(slab, _CONV_SLAB - res, axis=0)
                for o in taps:
                    q = 8 * (o // 8)
                    term = sh[q:q + _CONV_ROWS].reshape(groups, 8, _LANES) * wb_ref[o - off][:, c0:c0 + _LANES]
                    acc = term if acc is None else acc + term
            dst_ref[pl.ds(r0, _CONV_ROWS), c0:c0 + _LANES] = acc.reshape(_CONV_ROWS, _LANES)
        return 0

    lax.fori_loop(0, t // _CONV_ROWS, tile, 0)


def _conformer_kernel(vp, gp, vc, gc, vn, gn, s_ref, wb_ref, prm_ref, o_ref, win_ref, conv_ref, *, t, n_taps):
    _fill_window(win_ref, lambda v, g: v * _sigmoid(g), (vp, gp), (vc, gc), (vn, gn), t)
    _dw_conv(win_ref, wb_ref, conv_ref, t, n_taps)

    def tile(ti, _):
        r0 = pl.multiple_of(ti * _EPI_ROWS, _EPI_ROWS)
        acc = conv_ref[pl.ds(r0, _EPI_ROWS), :] + prm_ref[0:1, :]
        mu = jnp.mean(acc, axis=-1, keepdims=True)
        d = acc - mu
        var = jnp.mean(d * d, axis=-1, keepdims=True)
        yn = d * lax.rsqrt(var + EPS) * prm_ref[1:2, :] + prm_ref[2:3, :]
        s = s_ref[pl.ds(r0, _EPI_ROWS), :].astype(F32)
        o_ref[pl.ds(r0, _EPI_ROWS), :] = (_silu(yn) * _silu(s)).astype(o_ref.dtype)
        return 0

    trips = t // _EPI_ROWS
    lax.fori_loop(0, trips, tile, 0, unroll=2 if trips % 2 == 0 else 1)


def _shortconv_kernel(xp, cp, xc, cc, xn, cn, gb_ref, s_ref, wb_ref, o_ref, win_ref, conv_ref, *, t, n_taps):
    _fill_window(win_ref, lambda x, c: c * x, (xp, cp), (xc, cc), (xn, cn), t)
    _dw_conv(win_ref, wb_ref, conv_ref, t, n_taps)

    def tile(ti, _):
        r0 = pl.multiple_of(ti * _EPI_ROWS, _EPI_ROWS)
        gb = gb_ref[pl.ds(r0, _EPI_ROWS), :].astype(F32)
        s = s_ref[pl.ds(r0, _EPI_ROWS), :].astype(F32)
        o_ref[pl.ds(r0, _EPI_ROWS), :] = (gb * conv_ref[pl.ds(r0, _EPI_ROWS), :] * _silu(s)).astype(o_ref.dtype)
        return 0

    trips = t // _EPI_ROWS
    lax.fori_loop(0, trips, tile, 0, unroll=2 if trips % 2 == 0 else 1)


def _sublane_repeat(w):
    return jnp.broadcast_to(w.astype(F32)[:, None, :], (w.shape[0], 8, w.shape[1]))


def _conv_scratch(t, width):
    return [pltpu.VMEM((t + 2 * BF16_SUBLANES, width), F32), pltpu.VMEM((t, width), F32)]


def _conformer_mixer(z, w_conv, prm, width, col_val, col_gl, col_s):
    b, l, _ = z.shape
    t = _row_tile(l, 512)
    nb = l // t
    n_taps = w_conv.shape[0]
    wb = _sublane_repeat(w_conv)
    vp, vc, vn = _halo_specs(t, width, col_val, nb)
    gp, gc, gn = _halo_specs(t, width, col_gl, nb)
    return pl.pallas_call(
        functools.partial(_conformer_kernel, t=t, n_taps=n_taps),
        grid=(b, nb),
        in_specs=[vp, gp, vc, gc, vn, gn,
                  pl.BlockSpec((None, t, width), lambda bi, i: (bi, i, col_s)),
                  pl.BlockSpec(wb.shape, lambda bi, i: (0, 0, 0)),
                  pl.BlockSpec(prm.shape, lambda bi, i: (0, 0))],
        out_specs=pl.BlockSpec((None, t, width), lambda bi, i: (bi, i, 0)),
        out_shape=jax.ShapeDtypeStruct((b, l, width), BF16),
        scratch_shapes=_conv_scratch(t, width),
        compiler_params=_params("parallel", "parallel"),
        name="conformer_conv",
    )(z, z, z, z, z, z, z, wb, prm)


def _shortconv_mixer(z, w_conv, width, col_x, col_gb, col_gc, col_s):
    b, l, _ = z.shape
    t = _row_tile(l, 512)
    nb = l // t
    n_taps = w_conv.shape[0]
    wb = _sublane_repeat(w_conv)
    xp, xc, xn = _halo_specs(t, width, col_x, nb)
    cp, cc, cn = _halo_specs(t, width, col_gc, nb)
    return pl.pallas_call(
        functools.partial(_shortconv_kernel, t=t, n_taps=n_taps),
        grid=(b, nb),
        in_specs=[xp, cp, xc, cc, xn, cn,
                  pl.BlockSpec((None, t, width), lambda bi, i: (bi, i, col_gb)),
                  pl.BlockSpec((None, t, width), lambda bi, i: (bi, i, col_s)),
                  pl.BlockSpec(wb.shape, lambda bi, i: (0, 0, 0))],
        out_specs=pl.BlockSpec((None, t, width), lambda bi, i: (bi, i, 0)),
        out_shape=jax.ShapeDtypeStruct((b, l, width), BF16),
        scratch_shapes=_conv_scratch(t, width),
        compiler_params=_params("parallel", "parallel"),
        name="short_conv",
    )(z, z, z, z, z, z, z, z, wb)


def _dft_split(n):
    n2 = 1 << (int(math.log2(n)) // 2)
    n1 = n // n2
    assert n1 * n2 == n
    return n1, n2


def _cos_sin(rows, cols, period):
    ang = 2.0 * np.pi * ((np.outer(np.arange(rows), np.arange(cols))) % period) / period
    return np.cos(ang), np.sin(ang)


def _fnet_tables(seq, group_width):
    n1, n2 = _dft_split(seq)
    cc, sc = _cos_sin(group_width, group_width, group_width)
    c1, s1 = _cos_sin(n1, n1, n1)
    c2, s2 = _cos_sin(n2, n2, n2)
    tc, ts = _cos_sin(n2, n1, seq)
    lanes = 128
    tables = dict(
        cs=np.concatenate([cc, sc], axis=1),
        m1=np.block([[c1, -s1], [s1, c1]]),
        m2=np.concatenate([c2, -s2], axis=1),
        tc=np.broadcast_to(tc[:, :, None], (n2, n1, lanes)),
        ts=np.broadcast_to(ts[:, :, None], (n2, n1, lanes)),
    )
    out = {k: jnp.asarray(np.ascontiguousarray(v), F32) for k, v in tables.items()}
    for k in ("cs", "m1", "m2"):
        out[k] = out[k].astype(BF16)
    return out


def _fnet_stage1_kernel(x_ref, cs_ref, m1_ref, tc_ref, ts_ref, or_ref, oi_ref, *, groups):
    n1, nb, width = x_ref.shape
    gw = width // groups
    x = pltpu.einshape("abc->bac", x_ref[...]).reshape(nb * n1, width)
    ps, qs = [], []
    for g in range(groups):
        pq = jnp.dot(x[:, g * gw:(g + 1) * gw], cs_ref[...], preferred_element_type=F32)
        ps.append(pq[:, :gw])
        qs.append(pq[:, gw:])
    p = jnp.concatenate(ps, axis=1).astype(BF16)
    q = jnp.concatenate(qs, axis=1).astype(BF16)
    reps = width // tc_ref.shape[-1]
    res_r, res_i = [], []
    for j in range(nb):
        pq = jnp.concatenate([p[j * n1:(j + 1) * n1], q[j * n1:(j + 1) * n1]], axis=0)
        a = jnp.dot(m1_ref[...], pq, preferred_element_type=F32)
        ar, ai = a[:n1], a[n1:]
        tc = jnp.tile(tc_ref[j], (1, reps))
        ts = jnp.tile(ts_ref[j], (1, reps))
        res_r.append((ar * tc - ai * ts).astype(or_ref.dtype))
        res_i.append((ar * ts + ai * tc).astype(oi_ref.dtype))
    or_ref[...] = pltpu.einshape("abc->bac", jnp.stack(res_r, axis=0))
    oi_ref[...] = pltpu.einshape("abc->bac", jnp.stack(res_i, axis=0))


def _fnet_stage2_kernel(ar_ref, ai_ref, m2_ref, s_ref, o_ref, *, scale):
    kb = ar_ref.shape[0]
    res = []
    for j in range(kb):
        rhs = jnp.concatenate([ar_ref[j], ai_ref[j]], axis=0)
        res.append(jnp.dot(m2_ref[...], rhs, preferred_element_type=F32) * scale)
    f = pltpu.einshape("abc->bac", jnp.stack(res, axis=0))
    o_ref[...] = (f * _silu(s_ref[...].astype(F32))).astype(o_ref.dtype)


def _fnet_mixer(zd, z, col_s, tables):
    b, l, width = zd.shape
    n1, n2 = _dft_split(l)
    gw = width // FNET_GROUPS
    nb = min(n2, BF16_SUBLANES)
    kb = min(n1, BF16_SUBLANES)
    x4 = zd.reshape(b, n1, n2, width)
    blk = pl.BlockSpec((None, n1, nb, width), lambda bi, j: (bi, 0, j, 0))
    tw = pl.BlockSpec((nb, n1, 128), lambda bi, j: (j, 0, 0))
    full = lambda a: pl.BlockSpec(a.shape, lambda bi, j: (0,) * a.ndim)
    ar, ai = pl.pallas_call(
        functools.partial(_fnet_stage1_kernel, groups=FNET_GROUPS),
        grid=(b, n2 // nb),
        in_specs=[blk, full(tables["cs"]), full(tables["m1"]), tw, tw],
        out_specs=[blk, blk],
        out_shape=[jax.ShapeDtypeStruct((b, n1, n2, width), BF16)] * 2,
        compiler_params=_params("parallel", "parallel"),
        name="fnet_stage1",
    )(x4, tables["cs"], tables["m1"], tables["tc"], tables["ts"])
    blk2 = pl.BlockSpec((None, kb, n2, width), lambda bi, j: (bi, j, 0, 0))
    z4 = z.reshape(b, n2, n1, z.shape[-1])
    scale = 1.0 / math.sqrt(l * gw)
    f = pl.pallas_call(
        functools.partial(_fnet_stage2_kernel, scale=scale),
        grid=(b, n1 // kb),
        in_specs=[blk2, blk2, full(tables["m2"]),
                  pl.BlockSpec((None, n2, kb, width), lambda bi, j: (bi, 0, j, col_s))],
        out_specs=pl.BlockSpec((None, n2, kb, width), lambda bi, j: (bi, 0, j, 0)),
        out_shape=jax.ShapeDtypeStruct((b, n2, n1, width), BF16),
        compiler_params=_params("parallel", "parallel"),
        name="fnet_stage2",
    )(ar, ai, tables["m2"], z4)
    return f.reshape(b, l, width)


def _branch_kernel(pa, pb, pc, pd, w_ref, ga, gb, gc, gd, o_ref):
    acc = None
    for k, (p, g) in enumerate(((pa, ga), (pb, gb), (pc, gc), (pd, gd))):
        br = jnp.dot(p[...], w_ref[k], preferred_element_type=F32)
        term = _sigmoid(g[...].astype(F32)) * br
        acc = term if acc is None else acc + term
    o_ref[...] = acc.astype(o_ref.dtype)


def _branch_merge(ps, w_branch, z2d, gate_col0, d_model):
    m, width = ps[0].shape
    tm = _row_tile(m, 1024)
    tn = _row_tile(d_model, 512)
    assert gate_col0 % tn == 0
    g0 = gate_col0 // tn
    per = d_model // tn
    p_spec = pl.BlockSpec((tm, width), lambda i, j: (i, 0))
    g_specs = [pl.BlockSpec((tm, tn), functools.partial(lambda i, j, k: (i, g0 + k * per + j), k=k))
               for k in range(N_BRANCH)]
    return pl.pallas_call(
        _branch_kernel,
        grid=(m // tm, d_model // tn),
        in_specs=[p_spec] * 4 + [pl.BlockSpec((N_BRANCH, width, tn), lambda i, j: (0, 0, j))] + g_specs,
        out_specs=pl.BlockSpec((tm, tn), lambda i, j: (i, j)),
        out_shape=jax.ShapeDtypeStruct((m, d_model), BF16),
        compiler_params=_params("parallel", "parallel"),
        name="branch_merge",
    )(*ps, w_branch, z2d, z2d, z2d, z2d)


def _out_kernel(m_ref, w_ref, x_ref, gate_ref, o_ref):
    out = jnp.dot(m_ref[...], w_ref[...], preferred_element_type=F32)
    o_ref[...] = x_ref[...] + gate_ref[...] * out


def _out_proj_residual(merged, w_out, x, gate):
    b, l, d = x.shape
    tm = _row_tile(l, 1024)
    tn = _row_tile(d, 1024)
    return pl.pallas_call(
        _out_kernel,
        grid=(b, l // tm, d // tn),
        in_specs=[pl.BlockSpec((None, tm, d), lambda bi, i, j: (bi, i, 0)),
                  pl.BlockSpec((d, tn), lambda bi, i, j: (0, j)),
                  pl.BlockSpec((None, tm, tn), lambda bi, i, j: (bi, i, j)),
                  pl.BlockSpec((None, 1, tn), lambda bi, i, j: (bi, 0, j))],
        out_specs=pl.BlockSpec((None, tm, tn), lambda bi, i, j: (bi, i, j)),
        out_shape=jax.ShapeDtypeStruct((b, l, d), x.dtype),
        compiler_params=_params("parallel", "parallel", "parallel"),
        name="out_proj_residual",
    )(merged, w_out, x, gate)


def _block_diag_gates(wa, wx, cb):
    depth, _, heads, hd, _ = wa.shape
    hpb = cb // hd
    n_cb = heads // hpb
    eye = jnp.eye(hpb, dtype=wa.dtype)

    def bd(w):
        w = w.reshape(depth, 2, n_cb, hpb, hd, hd)
        full = jnp.einsum("ldchij,hg->ldchigj", w, eye)
        return full.reshape(depth, 2, n_cb, cb, cb)

    both = jnp.concatenate([bd(wa), bd(wx)], axis=-1)
    return both.transpose(0, 2, 1, 3, 4).astype(BF16)


def _token_mix(z, zd, y_a, width, d_model, lp, tables):
    b, l, n_main = z.shape
    y_b = _conformer_mixer(z, lp["cf_conv_w"], lp["cf_prm"], width, 1, 2, 7)
    y_c = _shortconv_mixer(z, lp["sc_conv_w"], width, 3, 4, 5, 8)
    y_d = _fnet_mixer(zd, z, 9, tables)
    m = b * l
    ps = [y.reshape(m, width) for y in (y_a, y_b, y_c, y_d)]
    merged = _branch_merge(ps, lp["w_branch"], z.reshape(m, n_main), 10 * width, d_model)
    return merged.reshape(b, l, d_model)


def kernel(x, c, ctx, c_ctx, norm_g, w_mod, b_mod, w_in, lru_conv_w, lru_conv_b, lru_wa, lru_ba, lru_wx,
           lru_bx, lru_lambda, cf_conv_w, cf_conv_b, cf_ln_g, cf_ln_b, sc_conv_w, w_branch, w_out, final_g):
    bsz, seq, d_model = x.shape
    ctx_len = ctx.shape[1]
    depth = w_in.shape[0]
    width = d_model // N_BRANCH
    assert bsz + 1 <= 8

    four_blk = 6
    n_main_blk = w_in.shape[-1] // width - 1
    n_main = n_main_blk * width
    w_branch_b = w_branch.astype(BF16)
    w_out_b = w_out.astype(BF16)

    cb = min(256, width)
    wg = _block_diag_gates(lru_wa, lru_wx, cb)
    lru_prm = jnp.concatenate([lru_conv_w, lru_conv_b[:, :, None], lru_ba[:, :, None], lru_bx[:, :, None],
                               lru_lambda[:, :, None]], axis=2).astype(F32)
    cf_prm = jnp.pad(jnp.stack([cf_conv_b, cf_ln_g, cf_ln_b], axis=1), ((0, 0), (0, 5), (0, 0)))
    tables_lat = _fnet_tables(seq, width // FNET_GROUPS)
    tables_ctx = _fnet_tables(ctx_len, width // FNET_GROUPS)

    cin = jnp.concatenate([c, c_ctx[None], jnp.zeros((8 - bsz - 1, d_model), c.dtype)], axis=0)
    mods = _adaln_mod(cin, w_mod, b_mod)

    xc = ctx
    s_blk = (6 * width) // cb
    for l in range(depth):
        last = l == depth - 1
        lp = dict(cf_conv_w=cf_conv_w[l], cf_prm=cf_prm[l], sc_conv_w=sc_conv_w[l], w_branch=w_branch_b[l])
        g = norm_g[l][None]
        shift, scale, gate = (mods[l, :, k * d_model:(k + 1) * d_model] for k in range(3))
        rows_c = jnp.full((bsz,), bsz, jnp.int32)

        hc = _norm_mod(xc, g, scale[rows_c][:, None], shift[rows_c][:, None])
        hc2 = hc.reshape(bsz * ctx_len, d_model)
        zc = _in_proj(hc2, w_in, l, 0, 1 if last else n_main_blk, four_blk, width, "in_proj_ctx")
        zc = zc.reshape(bsz, ctx_len, -1)
        h0 = jnp.zeros((bsz, 2, width), F32)
        yc_a, h_ctx = _lru_mixer(zc, s_blk, wg[l], lru_prm[l], h0, width, with_out=not last)

        h = _norm_mod(x, g, scale[:bsz, None], shift[:bsz, None])
        h2 = h.reshape(bsz * seq, d_model)
        z = _in_proj(h2, w_in, l, 0, n_main_blk, four_blk, width, "in_proj").reshape(bsz, seq, n_main)
        zd = _in_proj(h2, w_in, l, four_blk, 1, None, width, "in_proj_fourier").reshape(bsz, seq, width)
        y_a, _ = _lru_mixer(z, s_blk, wg[l], lru_prm[l], h_ctx, width, with_out=True)
        merged = _token_mix(z, zd, y_a, width, d_model, lp, tables_lat)
        if not last:
            zcd = _in_proj(hc2, w_in, l, four_blk, 1, None, width, "in_proj_fourier_ctx")
            zcd = zcd.reshape(bsz, ctx_len, width)
            merged_c = _token_mix(zc, zcd, yc_a, width, d_model, lp, tables_ctx)
            xc = _out_proj_residual(merged_c, w_out_b[l], xc, gate[rows_c][:, None])
        x = _out_proj_residual(merged, w_out_b[l], x, gate[:bsz, None])
    return _final_norm(x, final_g[None])
```

```python
import functools
import math

import numpy as np
import jax
import jax.numpy as jnp
from jax import lax
from jax.experimental import pallas as pl
from jax.experimental.pallas import tpu as pltpu

EPS = 1e-6
LRU_C = 8.0
N_BRANCH = 4
FNET_GROUPS = 4
V7X_VMEM_LIMIT_BYTES = 56 * 1024 * 1024
BF16_SUBLANES = 16
F32 = jnp.float32
BF16 = jnp.bfloat16


def _params(*sem):
    return pltpu.CompilerParams(dimension_semantics=sem, vmem_limit_bytes=V7X_VMEM_LIMIT_BYTES)


def _sigmoid(v):
    return 0.5 * jnp.tanh(0.5 * v) + 0.5


def _silu(v):
    return v * _sigmoid(v)


def _mod_kernel(c_ref, w_ref, b_ref, o_ref):
    s = _silu(c_ref[...]).astype(BF16)
    w = w_ref[0].astype(BF16)
    o_ref[0] = jnp.dot(s, w, preferred_element_type=F32) + b_ref[0]


def _adaln_mod(cin, w_mod, b_mod):
    depth, d, n = w_mod.shape
    tn = min(512, n)
    return pl.pallas_call(
        _mod_kernel,
        grid=(depth, n // tn),
        in_specs=[pl.BlockSpec((8, d), lambda l, j: (0, 0)),
                  pl.BlockSpec((1, d, tn), lambda l, j: (l, 0, j)),
                  pl.BlockSpec((1, 1, tn), lambda l, j: (l, 0, j))],
        out_specs=pl.BlockSpec((1, 8, tn), lambda l, j: (l, 0, j)),
        out_shape=jax.ShapeDtypeStruct((depth, 8, n), F32),
        compiler_params=_params("parallel", "parallel"),
        name="adaln_mod",
    )(cin, w_mod, b_mod.reshape(depth, 1, n))


def _norm_mod_kernel(x_ref, g_ref, sc_ref, sh_ref, o_ref):
    x = x_ref[...]
    n = x * lax.rsqrt(jnp.mean(x * x, axis=-1, keepdims=True) + EPS) * g_ref[...]
    o_ref[...] = (n * (1.0 + sc_ref[...]) + sh_ref[...]).astype(o_ref.dtype)


def _norm_kernel(x_ref, g_ref, o_ref):
    x = x_ref[...]
    n = x * lax.rsqrt(jnp.mean(x * x, axis=-1, keepdims=True) + EPS) * g_ref[...]
    o_ref[...] = n.astype(o_ref.dtype)


def _row_tile(l, cap):
    t = min(l, cap)
    while l % t:
        t //= 2
    return t


def _norm_mod(x, g, scale, shift):
    b, l, d = x.shape
    tr = _row_tile(l, 256)
    return pl.pallas_call(
        _norm_mod_kernel,
        grid=(b, l // tr),
        in_specs=[pl.BlockSpec((None, tr, d), lambda bi, i: (bi, i, 0)),
                  pl.BlockSpec((1, d), lambda bi, i: (0, 0)),
                  pl.BlockSpec((None, 1, d), lambda bi, i: (bi, 0, 0)),
                  pl.BlockSpec((None, 1, d), lambda bi, i: (bi, 0, 0))],
        out_specs=pl.BlockSpec((None, tr, d), lambda bi, i: (bi, i, 0)),
        out_shape=jax.ShapeDtypeStruct((b, l, d), BF16),
        compiler_params=_params("parallel", "parallel"),
        name="norm_mod",
    )(x, g, scale, shift)


def _final_norm(x, g):
    b, l, d = x.shape
    tr = _row_tile(l, 256)
    return pl.pallas_call(
        _norm_kernel,
        grid=(b, l // tr),
        in_specs=[pl.BlockSpec((None, tr, d), lambda bi, i: (bi, i, 0)),
                  pl.BlockSpec((1, d), lambda bi, i: (0, 0))],
        out_specs=pl.BlockSpec((None, tr, d), lambda bi, i: (bi, i, 0)),
        out_shape=jax.ShapeDtypeStruct((b, l, d), x.dtype),
        compiler_params=_params("parallel", "parallel"),
        name="final_norm",
    )(x, g)


def _in_proj_kernel(h_ref, w_hbm, o_ref, wf_ref, wb_ref, sem, *, layer, first_blk, skip_blk):
    j = pl.program_id(0)
    i = pl.program_id(1)
    tn = wb_ref.shape[-1]
    n_slots = wf_ref.shape[0]

    def panel_copy(jj):
        blk = first_blk + jj
        if skip_blk is not None:
            blk = blk + jnp.where(blk >= skip_blk, 1, 0)
        col = pl.multiple_of(blk * tn, tn)
        slot = jj % n_slots
        return pltpu.make_async_copy(w_hbm.at[layer, :, pl.ds(col, tn)], wf_ref.at[slot], sem.at[slot])

    def start_next():
        @pl.when(j + 1 < pl.num_programs(0))
        def _():
            panel_copy(j + 1).start()

    @pl.when(i == 0)
    def _():
        @pl.when(j == 0)
        def _():
            panel_copy(j).start()

        if n_slots > 1:
            start_next()
        panel_copy(j).wait()
        wb_ref[...] = wf_ref[j % n_slots].astype(BF16)
        if n_slots == 1:
            start_next()

    o_ref[...] = jnp.dot(h_ref[...], wb_ref[...], preferred_element_type=F32).astype(o_ref.dtype)


def _in_proj_tiles_kernel(h_hbm, w_hbm, o_hbm, hbuf, obuf, wf_ref, wb_ref, hsem, osem, wsem, *,
                          layer, first_blk, skip_blk, n_tiles):
    j = pl.program_id(0)
    nj = pl.num_programs(0)
    tm = hbuf.shape[1]
    tn = wb_ref.shape[-1]

    def panel_copy(jj):
        blk = first_blk + jj
        if skip_blk is not None:
            blk = blk + jnp.where(blk >= skip_blk, 1, 0)
        col = pl.multiple_of(blk * tn, tn)
        return pltpu.make_async_copy(w_hbm.at[layer, :, pl.ds(col, tn)], wf_ref, wsem)

    def h_copy(i, slot):
        row = pl.multiple_of(i * tm, tm)
        return pltpu.make_async_copy(h_hbm.at[pl.ds(row, tm), :], hbuf.at[slot], hsem.at[slot])

    def o_copy(i, slot):
        row = pl.multiple_of(i * tm, tm)
        col = pl.multiple_of(j * tn, tn)
        return pltpu.make_async_copy(obuf.at[slot], o_hbm.at[pl.ds(row, tm), pl.ds(col, tn)], osem.at[slot])

    @pl.when(j == 0)
    def _():
        panel_copy(j).start()
        h_copy(0, 0).start()

    panel_copy(j).wait()
    wb_ref[...] = wf_ref[...].astype(BF16)

    @pl.when(j + 1 < nj)
    def _():
        panel_copy(j + 1).start()

    def pair(p, carry):
        for slot in (0, 1):
            i = 2 * p + slot
            h_copy(i, slot).wait()
            if slot == 0:
                h_copy(i + 1, 1).start()
            else:
                @pl.when(i + 1 < n_tiles)
                def _():
                    h_copy(i + 1, 0).start()

                @pl.when((i + 1 == n_tiles) & (j + 1 < nj))
                def _():
                    h_copy(0, 0).start()

            @pl.when(p > 0)
            def _():
                o_copy(i - 2, slot).wait()

            obuf[slot] = jnp.dot(hbuf[slot], wb_ref[...], preferred_element_type=F32).astype(obuf.dtype)
            o_copy(i, slot).start()
        return carry

    lax.fori_loop(0, n_tiles // 2, pair, 0)
    o_copy(n_tiles - 2, 0).wait()
    o_copy(n_tiles - 1, 1).wait()


def _in_proj_streamed(h, w_in, layer, first_blk, n_blk, skip_blk, tn, name):
    m, k = h.shape
    tm = _row_tile(m, 1024)
    n_tiles = m // tm
    assert n_tiles >= 2 and n_tiles % 2 == 0
    return pl.pallas_call(
        functools.partial(_in_proj_tiles_kernel, layer=layer, first_blk=first_blk, skip_blk=skip_blk,
                          n_tiles=n_tiles),
        grid=(n_blk,),
        in_specs=[pl.BlockSpec(memory_space=pl.ANY), pl.BlockSpec(memory_space=pl.ANY)],
        out_specs=pl.BlockSpec(memory_space=pl.ANY),
        out_shape=jax.ShapeDtypeStruct((m, n_blk * tn), BF16),
        scratch_shapes=[pltpu.VMEM((2, tm, k), BF16), pltpu.VMEM((2, tm, tn), BF16),
                        pltpu.VMEM((k, tn), F32), pltpu.VMEM((k, tn), BF16),
                        pltpu.SemaphoreType.DMA((2,)), pltpu.SemaphoreType.DMA((2,)),
                        pltpu.SemaphoreType.DMA(())],
        compiler_params=_params("arbitrary"),
        name=name,
    )(h, w_in)


def _in_proj(h, w_in, layer, first_blk, n_blk, skip_blk, tn, name):
    m, k = h.shape
    tm = _row_tile(m, 1024)
    if (m // tm) % 2 == 0:
        return _in_proj_streamed(h, w_in, layer, first_blk, n_blk, skip_blk, tn, name)
    n_slots = 2 if m == tm else 1
    return pl.pallas_call(
        functools.partial(_in_proj_kernel, layer=layer, first_blk=first_blk, skip_blk=skip_blk),
        grid=(n_blk, m // tm),
        in_specs=[pl.BlockSpec((tm, k), lambda j, i: (i, 0)),
                  pl.BlockSpec(memory_space=pl.ANY)],
        out_specs=pl.BlockSpec((tm, tn), lambda j, i: (i, j)),
        out_shape=jax.ShapeDtypeStruct((m, n_blk * tn), BF16),
        scratch_shapes=[pltpu.VMEM((n_slots, k, tn), F32), pltpu.VMEM((k, tn), BF16),
                        pltpu.SemaphoreType.DMA((n_slots,))],
        compiler_params=_params("arbitrary", "arbitrary"),
        name=name,
    )(h, w_in)


_LRU_ROW_CONV_B, _LRU_ROW_BA, _LRU_ROW_BX, _LRU_ROW_LAMBDA = 4, 5, 6, 7


def _lru_kernel(*refs, seq, chunk, n_conv, with_out):
    if with_out:
        x_ref, s_ref, wg_ref, prm_ref, h0_ref, y_ref, hfin_ref, hf_ref, hb_ref = refs
    else:
        x_ref, wg_ref, prm_ref, h0_ref, hfin_ref = refs
    cb = x_ref.shape[-1]
    n_chunks = seq // chunk
    groups = chunk // 8
    halo = BF16_SUBLANES
    row = lax.broadcasted_iota(jnp.int32, (groups, 8, cb), 1)

    def gates(u, d):
        prm = prm_ref[d]
        g = jnp.dot(u.astype(BF16), wg_ref[d], preferred_element_type=F32)
        t_r = jnp.tanh(g[:, :cb] + prm[_LRU_ROW_BA:_LRU_ROW_BA + 1])
        t_i = jnp.tanh(g[:, cb:] + prm[_LRU_ROW_BX:_LRU_ROW_BX + 1])
        lam = prm[_LRU_ROW_LAMBDA:_LRU_ROW_LAMBDA + 1]
        c_half = (0.5 * LRU_C) * (jnp.minimum(lam, 0.0) - jnp.log1p(jnp.exp(-jnp.abs(lam))))
        log_a = c_half * t_r + c_half
        a = jnp.exp(log_a)
        th = jnp.tanh(log_a)
        q = (-0.5 * th) / (1.0 - th)
        bb = jnp.sqrt(q) * (u * (t_i + 1.0))
        return a, bb

    def conv(c, d):
        base = pl.multiple_of(c * chunk, chunk)
        cur = x_ref[pl.ds(base, chunk), :].astype(F32)
        prm = prm_ref[d]
        if d == 0:
            pbase = pl.multiple_of(jnp.maximum(base - halo, 0), halo)
            edge = x_ref[pl.ds(pbase, halo), :].astype(F32) * jnp.where(c > 0, 1.0, 0.0)
            win = jnp.concatenate([edge, cur], axis=0)
            off = halo - (n_conv - 1)
        else:
            nbase = pl.multiple_of(jnp.minimum(base + chunk, seq - halo), halo)
            edge = x_ref[pl.ds(nbase, halo), :].astype(F32) * jnp.where(c < n_chunks - 1, 1.0, 0.0)
            win = jnp.concatenate([cur, edge], axis=0)
            off = 0
        u = jnp.broadcast_to(prm[_LRU_ROW_CONV_B:_LRU_ROW_CONV_B + 1], (chunk, cb))
        for k in range(n_conv):
            u = u + win[off + k:off + k + chunk] * prm[k:k + 1]
        return u

    def scan(a, bb, carry, c, d, dst_ref):
        a3 = a.reshape(groups, 8, cb)
        b3 = bb.reshape(groups, 8, cb)
        for s in (1, 2, 4):
            shift = s if d == 0 else 8 - s
            valid = (row >= s) if d == 0 else (row < 8 - s)
            a_sh = jnp.where(valid, pltpu.roll(a3, shift, axis=1), 1.0)
            b_sh = jnp.where(valid, pltpu.roll(b3, shift, axis=1), 0.0)
            b3 = a3 * b_sh + b3
            a3 = a3 * a_sh
        base = pl.multiple_of(c * chunk, chunk)
        order = range(groups) if d == 0 else range(groups - 1, -1, -1)
        for g in order:
            h = a3[g] * carry + b3[g]
            if dst_ref is not None:
                dst_ref[pl.ds(base + g * 8, 8), :] = h
            carry = h[7:8] if d == 0 else h[0:1]
        return carry

    def body(c, carry):
        cf, cbk = carry
        a, bb = gates(conv(c, 0), 0)
        cf = scan(a, bb, cf, c, 0, hf_ref if with_out else None)
        cr = n_chunks - 1 - c
        a, bb = gates(conv(cr, 1), 1)
        cbk = scan(a, bb, cbk, cr, 1, hb_ref if with_out else None)
        return cf, cbk

    h0 = h0_ref[...]
    cf, cbk = lax.fori_loop(0, n_chunks, body, (h0[0:1], h0[1:2]))
    hfin_ref[...] = jnp.concatenate([cf, cbk], axis=0)

    if with_out:
        def finish(c, _):
            base = pl.multiple_of(c * chunk, chunk)
            s = s_ref[pl.ds(base, chunk), :].astype(F32)
            y = hf_ref[pl.ds(base, chunk), :] + hb_ref[pl.ds(base, chunk), :]
            y_ref[pl.ds(base, chunk), :] = (y * _silu(s)).astype(y_ref.dtype)
            return 0
        lax.fori_loop(0, n_chunks, finish, 0)


def _lru_mixer(z, s_col_blk, wg, prm, h0, width, with_out):
    b, l, _ = z.shape
    n_cb, _, cb, _ = wg.shape
    n_conv = 4
    chunk = _row_tile(l, 256)
    kern = functools.partial(_lru_kernel, seq=l, chunk=chunk, n_conv=n_conv, with_out=with_out)
    x_spec = pl.BlockSpec((None, l, cb), lambda bi, ci: (bi, 0, ci))
    s_spec = pl.BlockSpec((None, l, cb), lambda bi, ci: (bi, 0, s_col_blk + ci))
    wg_spec = pl.BlockSpec((None, 2, cb, 2 * cb), lambda bi, ci: (ci, 0, 0, 0))
    prm_spec = pl.BlockSpec((2, 8, cb), lambda bi, ci: (0, 0, ci))
    st_spec = pl.BlockSpec((None, 2, cb), lambda bi, ci: (bi, 0, ci))
    st_shape = jax.ShapeDtypeStruct((b, 2, width), F32)
    if with_out:
        y, hfin = pl.pallas_call(
            kern, grid=(b, n_cb),
            in_specs=[x_spec, s_spec, wg_spec, prm_spec, st_spec],
            out_specs=[pl.BlockSpec((None, l, cb), lambda bi, ci: (bi, 0, ci)), st_spec],
            out_shape=[jax.ShapeDtypeStruct((b, l, width), BF16), st_shape],
            scratch_shapes=[pltpu.VMEM((l, cb), F32), pltpu.VMEM((l, cb), F32)],
            compiler_params=_params("parallel", "parallel"),
            name="rglru_mixer",
        )(z, z, wg, prm, h0)
        return y, hfin
    hfin = pl.pallas_call(
        kern, grid=(b, n_cb),
        in_specs=[x_spec, wg_spec, prm_spec, st_spec],
        out_specs=st_spec,
        out_shape=st_shape,
        compiler_params=_params("parallel", "parallel"),
        name="rglru_state",
    )(z, wg, prm, h0)
    return None, hfin


_CONV_ROWS = 64
_CONV_SLAB = 96
_EPI_ROWS = 64
_LANES = 128


def _halo_specs(t, width, col_blk, n_row_blocks):
    r = t // BF16_SUBLANES
    last = n_row_blocks * r - 1
    cur = pl.BlockSpec((None, t, width), lambda bi, i: (bi, i, col_blk))
    prev = pl.BlockSpec((None, BF16_SUBLANES, width), lambda bi, i: (bi, jnp.maximum(i * r - 1, 0), col_blk))
    nxt = pl.BlockSpec((None, BF16_SUBLANES, width), lambda bi, i: (bi, jnp.minimum((i + 1) * r, last), col_blk))
    return prev, cur, nxt


def _fill_window(win_ref, fn, prev_refs, cur_refs, next_refs, t):
    i = pl.program_id(1)
    n = pl.num_programs(1)
    h = BF16_SUBLANES
    load = lambda refs: [r[...].astype(F32) for r in refs]
    win_ref[0:h] = fn(*load(prev_refs)) * jnp.where(i > 0, 1.0, 0.0)
    win_ref[h:h + t] = fn(*load(cur_refs))
    win_ref[h + t:h + t + h] = fn(*load(next_refs)) * jnp.where(i < n - 1, 1.0, 0.0)


def _dw_conv(win_ref, wb_ref, dst_ref, t, n_taps):
    off = BF16_SUBLANES - n_taps // 2
    assert off >= 0 and off + n_taps - 1 + _CONV_ROWS <= _CONV_SLAB
    width = dst_ref.shape[-1]
    groups = _CONV_ROWS // 8

    def tile(ti, _):
        r0 = pl.multiple_of(ti * _CONV_ROWS, _CONV_ROWS)
        for c0 in range(0, width, _LANES):
            slab = win_ref[pl.ds(r0, _CONV_SLAB), c0:c0 + _LANES]
            acc = None
            for res in range(8):
                taps = [o for o in range(off, off + n_taps) if o % 8 == res]
                if not taps:
                    continue
                sh = slab if res == 0 else pltpu.roll(slab, _CONV_SLAB - res, axis=0)
                for o in taps:
                    q = 8 * (o // 8)
                    term = sh[q:q + _CONV_ROWS].reshape(groups, 8, _LANES) * wb_ref[o - off][:, c0:c0 + _LANES]
                    acc = term if acc is None else acc + term
            dst_ref[pl.ds(r0, _CONV_ROWS), c0:c0 + _LANES] = acc.reshape(_CONV_ROWS, _LANES)
        return 0

    lax.fori_loop(0, t // _CONV_ROWS, tile, 0)


def _conformer_kernel(vp, gp, vc, gc, vn, gn, s_ref, wb_ref, prm_ref, o_ref, win_ref, conv_ref, *, t, n_taps):
    _fill_window(win_ref, lambda v, g: v * _sigmoid(g), (vp, gp), (vc, gc), (vn, gn), t)
    _dw_conv(win_ref, wb_ref, conv_ref, t, n_taps)

    def tile(ti, _):
        r0 = pl.multiple_of(ti * _EPI_ROWS, _EPI_ROWS)
        acc = conv_ref[pl.ds(r0, _EPI_ROWS), :] + prm_ref[0:1, :]
        mu = jnp.mean(acc, axis=-1, keepdims=True)
        d = acc - mu
        var = jnp.mean(d * d, axis=-1, keepdims=True)
        yn = d * lax.rsqrt(var + EPS) * prm_ref[1:2, :] + prm_ref[2:3, :]
        s = s_ref[pl.ds(r0, _EPI_ROWS), :].astype(F32)
        o_ref[pl.ds(r0, _EPI_ROWS), :] = (_silu(yn) * _silu(s)).astype(o_ref.dtype)
        return 0

    trips = t // _EPI_ROWS
    lax.fori_loop(0, trips, tile, 0, unroll=2 if trips % 2 == 0 else 1)


def _shortconv_kernel(xp, cp, xc, cc, xn, cn, gb_ref, s_ref, wb_ref, o_ref, win_ref, conv_ref, *, t, n_taps):
    _fill_window(win_ref, lambda x, c: c * x, (xp, cp), (xc, cc), (xn, cn), t)
    _dw_conv(win_ref, wb_ref, conv_ref, t, n_taps)

    def tile(ti, _):
        r0 = pl.multiple_of(ti * _EPI_ROWS, _EPI_ROWS)
        gb = gb_ref[pl.ds(r0, _EPI_ROWS), :].astype(F32)
        s = s_ref[pl.ds(r0, _EPI_ROWS), :].astype(F32)
        o_ref[pl.ds(r0, _EPI_ROWS), :] = (gb * conv_ref[pl.ds(r0, _EPI_ROWS), :] * _silu(s)).astype(o_ref.dtype)
        return 0

    trips = t // _EPI_ROWS
    lax.fori_loop(0, trips, tile, 0, unroll=2 if trips % 2 == 0 else 1)


def _sublane_repeat(w):
    return jnp.broadcast_to(w.astype(F32)[:, None, :], (w.shape[0], 8, w.shape[1]))


def _conv_scratch(t, width):
    return [pltpu.VMEM((t + 2 * BF16_SUBLANES, width), F32), pltpu.VMEM((t, width), F32)]


def _conformer_mixer(z, w_conv, prm, width, col_val, col_gl, col_s):
    b, l, _ = z.shape
    t = _row_tile(l, 512)
    nb = l // t
    n_taps = w_conv.shape[0]
    wb = _sublane_repeat(w_conv)
    vp, vc, vn = _halo_specs(t, width, col_val, nb)
    gp, gc, gn = _halo_specs(t, width, col_gl, nb)
    return pl.pallas_call(
        functools.partial(_conformer_kernel, t=t, n_taps=n_taps),
        grid=(b, nb),
        in_specs=[vp, gp, vc, gc, vn, gn,
                  pl.BlockSpec((None, t, width), lambda bi, i: (bi, i, col_s)),
                  pl.BlockSpec(wb.shape, lambda bi, i: (0, 0, 0)),
                  pl.BlockSpec(prm.shape, lambda bi, i: (0, 0))],
        out_specs=pl.BlockSpec((None, t, width), lambda bi, i: (bi, i, 0)),
        out_shape=jax.ShapeDtypeStruct((b, l, width), BF16),
        scratch_shapes=_conv_scratch(t, width),
        compiler_params=_params("parallel", "parallel"),
        name="conformer_conv",
    )(z, z, z, z, z, z, z, wb, prm)


def _shortconv_mixer(z, w_conv, width, col_x, col_gb, col_gc, col_s):
    b, l, _ = z.shape
    t = _row_tile(l, 512)
    nb = l // t
    n_taps = w_conv.shape[0]
    wb = _sublane_repeat(w_conv)
    xp, xc, xn = _halo_specs(t, width, col_x, nb)
    cp, cc, cn = _halo_specs(t, width, col_gc, nb)
    return pl.pallas_call(
        functools.partial(_shortconv_kernel, t=t, n_taps=n_taps),
        grid=(b, nb),
        in_specs=[xp, cp, xc, cc, xn, cn,
                  pl.BlockSpec((None, t, width), lambda bi, i: (bi, i, col_gb)),
                  pl.BlockSpec((None, t, width), lambda bi, i: (bi, i, col_s)),
                  pl.BlockSpec(wb.shape, lambda bi, i: (0, 0, 0))],
        out_specs=pl.BlockSpec((None, t, width), lambda bi, i: (bi, i, 0)),
        out_shape=jax.ShapeDtypeStruct((b, l, width), BF16),
        scratch_shapes=_conv_scratch(t, width),
        compiler_params=_params("parallel", "parallel"),
        name="short_conv",
    )(z, z, z, z, z, z, z, z, wb)


def _dft_split(n):
    n2 = 1 << (int(math.log2(n)) // 2)
    n1 = n // n2
    assert n1 * n2 == n
    return n1, n2


def _cos_sin(rows, cols, period):
    ang = 2.0 * np.pi * ((np.outer(np.arange(rows), np.arange(cols))) % period) / period
    return np.cos(ang), np.sin(ang)


def _fnet_tables(seq, group_width):
    n1, n2 = _dft_split(seq)
    cc, sc = _cos_sin(group_width, group_width, group_width)
    c1, s1 = _cos_sin(n1, n1, n1)
    c2, s2 = _cos_sin(n2, n2, n2)
    tc, ts = _cos_sin(n2, n1, seq)
    lanes = 128
    tables = dict(
        cs=np.concatenate([cc, sc], axis=1),
        m1=np.block([[c1, -s1], [s1, c1]]),
        m2=np.concatenate([c2, -s2], axis=1),
        tc=np.broadcast_to(tc[:, :, None], (n2, n1, lanes)),
        ts=np.broadcast_to(ts[:, :, None], (n2, n1, lanes)),
    )
    out = {k: jnp.asarray(np.ascontiguousarray(v), F32) for k, v in tables.items()}
    for k in ("cs", "m1", "m2"):
        out[k] = out[k].astype(BF16)
    return out


def _fnet_stage1_kernel(x_ref, cs_ref, m1_ref, tc_ref, ts_ref, or_ref, oi_ref, *, groups):
    n1, nb, width = x_ref.shape
    gw = width // groups
    x = pltpu.einshape("abc->bac", x_ref[...]).reshape(nb * n1, width)
    ps, qs = [], []
    for g in range(groups):
        pq = jnp.dot(x[:, g * gw:(g + 1) * gw], cs_ref[...], preferred_element_type=F32)
        ps.append(pq[:, :gw])
        qs.append(pq[:, gw:])
    p = jnp.concatenate(ps, axis=1).astype(BF16)
    q = jnp.concatenate(qs, axis=1).astype(BF16)
    reps = width // tc_ref.shape[-1]
    res_r, res_i = [], []
    for j in range(nb):
        pq = jnp.concatenate([p[j * n1:(j + 1) * n1], q[j * n1:(j + 1) * n1]], axis=0)
        a = jnp.dot(m1_ref[...], pq, preferred_element_type=F32)
        ar, ai = a[:n1], a[n1:]
        tc = jnp.tile(tc_ref[j], (1, reps))
        ts = jnp.tile(ts_ref[j], (1, reps))
        res_r.append((ar * tc - ai * ts).astype(or_ref.dtype))
        res_i.append((ar * ts + ai * tc).astype(oi_ref.dtype))
    or_ref[...] = pltpu.einshape("abc->bac", jnp.stack(res_r, axis=0))
    oi_ref[...] = pltpu.einshape("abc->bac", jnp.stack(res_i, axis=0))


def _fnet_stage2_kernel(ar_ref, ai_ref, m2_ref, s_ref, o_ref, *, scale):
    kb = ar_ref.shape[0]
    res = []
    for j in range(kb):
        rhs = jnp.concatenate([ar_ref[j], ai_ref[j]], axis=0)
        res.append(jnp.dot(m2_ref[...], rhs, preferred_element_type=F32) * scale)
    f = pltpu.einshape("abc->bac", jnp.stack(res, axis=0))
    o_ref[...] = (f * _silu(s_ref[...].astype(F32))).astype(o_ref.dtype)


def _fnet_mixer(zd, z, col_s, tables):
    b, l, width = zd.shape
    n1, n2 = _dft_split(l)
    gw = width // FNET_GROUPS
    nb = min(n2, BF16_SUBLANES)
    kb = min(n1, BF16_SUBLANES)
    x4 = zd.reshape(b, n1, n2, width)
    blk = pl.BlockSpec((None, n1, nb, width), lambda bi, j: (bi, 0, j, 0))
    tw = pl.BlockSpec((nb, n1, 128), lambda bi, j: (j, 0, 0))
    full = lambda a: pl.BlockSpec(a.shape, lambda bi, j: (0,) * a.ndim)
    ar, ai = pl.pallas_call(
        functools.partial(_fnet_stage1_kernel, groups=FNET_GROUPS),
        grid=(b, n2 // nb),
        in_specs=[blk, full(tables["cs"]), full(tables["m1"]), tw, tw],
        out_specs=[blk, blk],
        out_shape=[jax.ShapeDtypeStruct((b, n1, n2, width), BF16)] * 2,
        compiler_params=_params("parallel", "parallel"),
        name="fnet_stage1",
    )(x4, tables["cs"], tables["m1"], tables["tc"], tables["ts"])
    blk2 = pl.BlockSpec((None, kb, n2, width), lambda bi, j: (bi, j, 0, 0))
    z4 = z.reshape(b, n2, n1, z.shape[-1])
    scale = 1.0 / math.sqrt(l * gw)
    f = pl.pallas_call(
        functools.partial(_fnet_stage2_kernel, scale=scale),
        grid=(b, n1 // kb),
        in_specs=[blk2, blk2, full(tables["m2"]),
                  pl.BlockSpec((None, n2, kb, width), lambda bi, j: (bi, 0, j, col_s))],
        out_specs=pl.BlockSpec((None, n2, kb, width), lambda bi, j: (bi, 0, j, 0)),
        out_shape=jax.ShapeDtypeStruct((b, n2, n1, width), BF16),
        compiler_params=_params("parallel", "parallel"),
        name="fnet_stage2",
    )(ar, ai, tables["m2"], z4)
    return f.reshape(b, l, width)


def _branch_kernel(pa, pb, pc, pd, w_ref, ga, gb, gc, gd, o_ref):
    acc = None
    for k, (p, g) in enumerate(((pa, ga), (pb, gb), (pc, gc), (pd, gd))):
        br = jnp.dot(p[...], w_ref[k], preferred_element_type=F32)
        term = _sigmoid(g[...].astype(F32)) * br
        acc = term if acc is None else acc + term
    o_ref[...] = acc.astype(o_ref.dtype)


def _branch_merge(ps, w_branch, layer, z2d, gate_col0, d_model):
    m, width = ps[0].shape
    tm = _row_tile(m, 1024)
    tn = _row_tile(d_model, 512)
    assert gate_col0 % tn == 0
    g0 = gate_col0 // tn
    per = d_model // tn
    p_spec = pl.BlockSpec((tm, width), lambda i, j: (i, 0))
    g_specs = [pl.BlockSpec((tm, tn), functools.partial(lambda i, j, k: (i, g0 + k * per + j), k=k))
               for k in range(N_BRANCH)]
    return pl.pallas_call(
        _branch_kernel,
        grid=(m // tm, d_model // tn),
        in_specs=[p_spec] * 4 + [pl.BlockSpec((None, N_BRANCH, width, tn), lambda i, j: (layer, 0, 0, j))] + g_specs,
        out_specs=pl.BlockSpec((tm, tn), lambda i, j: (i, j)),
        out_shape=jax.ShapeDtypeStruct((m, d_model), BF16),
        compiler_params=_params("parallel", "parallel"),
        name="branch_merge",
    )(*ps, w_branch, z2d, z2d, z2d, z2d)


def _out_kernel(m_ref, w_ref, x_ref, gate_ref, o_ref):
    out = jnp.dot(m_ref[...], w_ref[...], preferred_element_type=F32)
    o_ref[...] = x_ref[...] + gate_ref[...] * out


def _out_proj_residual(merged, w_out, layer, x, gate):
    b, l, d = x.shape
    tm = _row_tile(l, 1024)
    tn = _row_tile(d, 1024)
    return pl.pallas_call(
        _out_kernel,
        grid=(b, l // tm, d // tn),
        in_specs=[pl.BlockSpec((None, tm, d), lambda bi, i, j: (bi, i, 0)),
                  pl.BlockSpec((None, d, tn), lambda bi, i, j: (layer, 0, j)),
                  pl.BlockSpec((None, tm, tn), lambda bi, i, j: (bi, i, j)),
                  pl.BlockSpec((None, 1, tn), lambda bi, i, j: (bi, 0, j))],
        out_specs=pl.BlockSpec((None, tm, tn), lambda bi, i, j: (bi, i, j)),
        out_shape=jax.ShapeDtypeStruct((b, l, d), x.dtype),
        compiler_params=_params("parallel", "parallel", "parallel"),
        name="out_proj_residual",
    )(merged, w_out, x, gate)


def _block_diag_gates(wa, wx, cb):
    depth, _, heads, hd, _ = wa.shape
    hpb = cb // hd
    n_cb = heads // hpb
    eye = jnp.eye(hpb, dtype=wa.dtype)

    def bd(w):
        w = w.reshape(depth, 2, n_cb, hpb, hd, hd)
        full = jnp.einsum("ldchij,hg->ldchigj", w, eye)
        return full.reshape(depth, 2, n_cb, cb, cb)

    both = 0.5 * jnp.concatenate([bd(wa), bd(wx)], axis=-1)
    return both.transpose(0, 2, 1, 3, 4).astype(BF16)


def _token_mix(z, zd, y_a, width, d_model, lp, tables):
    b, l, n_main = z.shape
    y_b = _conformer_mixer(z, lp["cf_conv_w"], lp["cf_prm"], width, 1, 2, 7)
    y_c = _shortconv_mixer(z, lp["sc_conv_w"], width, 3, 4, 5, 8)
    y_d = _fnet_mixer(zd, z, 9, tables)
    m = b * l
    ps = [y.reshape(m, width) for y in (y_a, y_b, y_c, y_d)]
    merged = _branch_merge(ps, lp["w_branch"], lp["layer"], z.reshape(m, n_main), 10 * width, d_model)
    return merged.reshape(b, l, d_model)


def kernel(x, c, ctx, c_ctx, norm_g, w_mod, b_mod, w_in, lru_conv_w, lru_conv_b, lru_wa, lru_ba, lru_wx,
           lru_bx, lru_lambda, cf_conv_w, cf_conv_b, cf_ln_g, cf_ln_b, sc_conv_w, w_branch, w_out, final_g):
    bsz, seq, d_model = x.shape
    ctx_len = ctx.shape[1]
    depth = w_in.shape[0]
    width = d_model // N_BRANCH
    assert bsz + 1 <= 8

    four_blk = 6
    n_main_blk = w_in.shape[-1] // width - 1
    n_main = n_main_blk * width
    w_branch_b = w_branch.astype(BF16)
    w_out_b = w_out.astype(BF16)

    cb = min(256, width)
    wg = _block_diag_gates(lru_wa, lru_wx, cb)
    lru_prm = jnp.concatenate([lru_conv_w, lru_conv_b[:, :, None], 0.5 * lru_ba[:, :, None],
                               0.5 * lru_bx[:, :, None], lru_lambda[:, :, None]], axis=2).astype(F32)
    cf_prm = jnp.pad(jnp.stack([cf_conv_b, cf_ln_g, cf_ln_b], axis=1), ((0, 0), (0, 5), (0, 0)))
    tables_lat = _fnet_tables(seq, width // FNET_GROUPS)
    tables_ctx = _fnet_tables(ctx_len, width // FNET_GROUPS)

    cin = jnp.concatenate([c, c_ctx[None], jnp.zeros((8 - bsz - 1, d_model), c.dtype)], axis=0)
    mods = _adaln_mod(cin, w_mod, b_mod)

    xc = ctx
    s_blk = (6 * width) // cb
    for l in range(depth):
        last = l == depth - 1
        lp = dict(cf_conv_w=cf_conv_w[l], cf_prm=cf_prm[l], sc_conv_w=sc_conv_w[l], w_branch=w_branch_b, layer=l)
        g = norm_g[l][None]
        shift, scale, gate = (mods[l, :, k * d_model:(k + 1) * d_model] for k in range(3))
        rows_c = jnp.full((bsz,), bsz, jnp.int32)

        hc = _norm_mod(xc, g, scale[rows_c][:, None], shift[rows_c][:, None])
        hc2 = hc.reshape(bsz * ctx_len, d_model)
        zc = _in_proj(hc2, w_in, l, 0, 1 if last else n_main_blk, four_blk, width, "in_proj_ctx")
        zc = zc.reshape(bsz, ctx_len, -1)
        h0 = jnp.zeros((bsz, 2, width), F32)
        yc_a, h_ctx = _lru_mixer(zc, s_blk, wg[l], lru_prm[l], h0, width, with_out=not last)

        h = _norm_mod(x, g, scale[:bsz, None], shift[:bsz, None])
        h2 = h.reshape(bsz * seq, d_model)
        z = _in_proj(h2, w_in, l, 0, n_main_blk, four_blk, width, "in_proj").reshape(bsz, seq, n_main)
        zd = _in_proj(h2, w_in, l, four_blk, 1, None, width, "in_proj_fourier").reshape(bsz, seq, width)
        y_a, _ = _lru_mixer(z, s_blk, wg[l], lru_prm[l], h_ctx, width, with_out=True)
        merged = _token_mix(z, zd, y_a, width, d_model, lp, tables_lat)
        if not last:
            zcd = _in_proj(hc2, w_in, l, four_blk, 1, None, width, "in_proj_fourier_ctx")
            zcd = zcd.reshape(bsz, ctx_len, width)
            merged_c = _token_mix(zc, zcd, yc_a, width, d_model, lp, tables_ctx)
            xc = _out_proj_residual(merged_c, w_out_b, l, xc, gate[rows_c][:, None])
        x = _out_proj_residual(merged, w_out_b, l, x, gate[:bsz, None])
    return _final_norm(x, final_g[None])
```

```python
import functools
import math

import numpy as np
import jax
import jax.numpy as jnp
from jax import lax
from jax.experimental import pallas as pl
from jax.experimental.pallas import tpu as pltpu

EPS = 1e-6
LRU_C = 8.0
N_BRANCH = 4
FNET_GROUPS = 4
V7X_VMEM_LIMIT_BYTES = 56 * 1024 * 1024
BF16_SUBLANES = 16
F32 = jnp.float32
BF16 = jnp.bfloat16


def _params(*sem):
    return pltpu.CompilerParams(dimension_semantics=sem, vmem_limit_bytes=V7X_VMEM_LIMIT_BYTES)


def _sigmoid(v):
    return 0.5 * jnp.tanh(0.5 * v) + 0.5


def _silu(v):
    return v * _sigmoid(v)


def _mod_kernel(c_ref, w_ref, b_ref, o_ref):
    s = _silu(c_ref[...]).astype(BF16)
    w = w_ref[0].astype(BF16)
    o_ref[0] = jnp.dot(s, w, preferred_element_type=F32) + b_ref[0]


def _adaln_mod(cin, w_mod, b_mod):
    depth, d, n = w_mod.shape
    tn = min(512, n)
    return pl.pallas_call(
        _mod_kernel,
        grid=(depth, n // tn),
        in_specs=[pl.BlockSpec((8, d), lambda l, j: (0, 0)),
                  pl.BlockSpec((1, d, tn), lambda l, j: (l, 0, j)),
                  pl.BlockSpec((1, 1, tn), lambda l, j: (l, 0, j))],
        out_specs=pl.BlockSpec((1, 8, tn), lambda l, j: (l, 0, j)),
        out_shape=jax.ShapeDtypeStruct((depth, 8, n), F32),
        compiler_params=_params("parallel", "parallel"),
        name="adaln_mod",
    )(cin, w_mod, b_mod.reshape(depth, 1, n))


def _norm_mod_kernel(x_ref, g_ref, sc_ref, sh_ref, o_ref):
    x = x_ref[...]
    n = x * lax.rsqrt(jnp.mean(x * x, axis=-1, keepdims=True) + EPS) * g_ref[...]
    o_ref[...] = (n * (1.0 + sc_ref[...]) + sh_ref[...]).astype(o_ref.dtype)


def _row_tile(l, cap):
    t = min(l, cap)
    while l % t:
        t //= 2
    return t


def _norm_mod(x, g, scale, shift):
    b, l, d = x.shape
    tr = _row_tile(l, 256)
    return pl.pallas_call(
        _norm_mod_kernel,
        grid=(b, l // tr),
        in_specs=[pl.BlockSpec((None, tr, d), lambda bi, i: (bi, i, 0)),
                  pl.BlockSpec((1, d), lambda bi, i: (0, 0)),
                  pl.BlockSpec((None, 1, d), lambda bi, i: (bi, 0, 0)),
                  pl.BlockSpec((None, 1, d), lambda bi, i: (bi, 0, 0))],
        out_specs=pl.BlockSpec((None, tr, d), lambda bi, i: (bi, i, 0)),
        out_shape=jax.ShapeDtypeStruct((b, l, d), BF16),
        compiler_params=_params("parallel", "parallel"),
        name="norm_mod",
    )(x, g, scale, shift)


def _in_proj_kernel(h_ref, w_hbm, o_ref, wf_ref, wb_ref, sem, *, layer, first_blk, skip_blk):
    j = pl.program_id(0)
    i = pl.program_id(1)
    tn = wb_ref.shape[-1]
    n_slots = wf_ref.shape[0]

    def panel_copy(jj):
        blk = first_blk + jj
        if skip_blk is not None:
            blk = blk + jnp.where(blk >= skip_blk, 1, 0)
        col = pl.multiple_of(blk * tn, tn)
        slot = jj % n_slots
        return pltpu.make_async_copy(w_hbm.at[layer, :, pl.ds(col, tn)], wf_ref.at[slot], sem.at[slot])

    def start_next():
        @pl.when(j + 1 < pl.num_programs(0))
        def _():
            panel_copy(j + 1).start()

    @pl.when(i == 0)
    def _():
        @pl.when(j == 0)
        def _():
            panel_copy(j).start()

        if n_slots > 1:
            start_next()
        panel_copy(j).wait()
        wb_ref[...] = wf_ref[j % n_slots].astype(BF16)
        if n_slots == 1:
            start_next()

    o_ref[...] = jnp.dot(h_ref[...], wb_ref[...], preferred_element_type=F32).astype(o_ref.dtype)


def _in_proj(h, w_in, layer, first_blk, n_blk, skip_blk, tn, name):
    m, k = h.shape
    tm = _row_tile(m, 1024)
    n_slots = 2 if m == tm else 1
    return pl.pallas_call(
        functools.partial(_in_proj_kernel, layer=layer, first_blk=first_blk, skip_blk=skip_blk),
        grid=(n_blk, m // tm),
        in_specs=[pl.BlockSpec((tm, k), lambda j, i: (i, 0)),
                  pl.BlockSpec(memory_space=pl.ANY)],
        out_specs=pl.BlockSpec((tm, tn), lambda j, i: (i, j)),
        out_shape=jax.ShapeDtypeStruct((m, n_blk * tn), BF16),
        scratch_shapes=[pltpu.VMEM((n_slots, k, tn), F32), pltpu.VMEM((k, tn), BF16),
                        pltpu.SemaphoreType.DMA((n_slots,))],
        compiler_params=_params("arbitrary", "arbitrary"),
        name=name,
    )(h, w_in)


_LRU_ROW_CONV_B, _LRU_ROW_BA, _LRU_ROW_BX, _LRU_ROW_LAMBDA = 4, 5, 6, 7


def _lru_kernel(*refs, seq, chunk, n_conv, with_out):
    if with_out:
        x_ref, s_ref, wg_ref, prm_ref, h0_ref, y_ref, hfin_ref, hf_ref, hb_ref = refs
    else:
        x_ref, wg_ref, prm_ref, h0_ref, hfin_ref = refs
    cb = x_ref.shape[-1]
    n_chunks = seq // chunk
    groups = chunk // 8
    halo = BF16_SUBLANES
    row = lax.broadcasted_iota(jnp.int32, (groups, 8, cb), 1)

    def gates(u, d):
        prm = prm_ref[d]
        g = jnp.dot(u.astype(BF16), wg_ref[d], preferred_element_type=F32)
        t_r = jnp.tanh(g[:, :cb] + prm[_LRU_ROW_BA:_LRU_ROW_BA + 1])
        t_i = jnp.tanh(g[:, cb:] + prm[_LRU_ROW_BX:_LRU_ROW_BX + 1])
        lam = prm[_LRU_ROW_LAMBDA:_LRU_ROW_LAMBDA + 1]
        c_half = (0.5 * LRU_C) * (jnp.minimum(lam, 0.0) - jnp.log1p(jnp.exp(-jnp.abs(lam))))
        log_a = c_half * t_r + c_half
        a = jnp.exp(log_a)
        th = jnp.tanh(log_a)
        q = (-0.5 * th) / (1.0 - th)
        bb = jnp.sqrt(q) * (u * (t_i + 1.0))
        return a, bb

    def conv(c, d):
        base = pl.multiple_of(c * chunk, chunk)
        cur = x_ref[pl.ds(base, chunk), :].astype(F32)
        prm = prm_ref[d]
        if d == 0:
            pbase = pl.multiple_of(jnp.maximum(base - halo, 0), halo)
            edge = x_ref[pl.ds(pbase, halo), :].astype(F32) * jnp.where(c > 0, 1.0, 0.0)
            win = jnp.concatenate([edge, cur], axis=0)
            off = halo - (n_conv - 1)
        else:
            nbase = pl.multiple_of(jnp.minimum(base + chunk, seq - halo), halo)
            edge = x_ref[pl.ds(nbase, halo), :].astype(F32) * jnp.where(c < n_chunks - 1, 1.0, 0.0)
            win = jnp.concatenate([cur, edge], axis=0)
            off = 0
        u = jnp.broadcast_to(prm[_LRU_ROW_CONV_B:_LRU_ROW_CONV_B + 1], (chunk, cb))
        for k in range(n_conv):
            u = u + win[off + k:off + k + chunk] * prm[k:k + 1]
        return u

    def scan(a, bb, carry, c, d, dst_ref):
        a3 = a.reshape(groups, 8, cb)
        b3 = bb.reshape(groups, 8, cb)
        for s in (1, 2, 4):
            shift = s if d == 0 else 8 - s
            valid = (row >= s) if d == 0 else (row < 8 - s)
            a_sh = jnp.where(valid, pltpu.roll(a3, shift, axis=1), 1.0)
            b_sh = jnp.where(valid, pltpu.roll(b3, shift, axis=1), 0.0)
            b3 = a3 * b_sh + b3
            a3 = a3 * a_sh
        base = pl.multiple_of(c * chunk, chunk)
        order = range(groups) if d == 0 else range(groups - 1, -1, -1)
        for g in order:
            h = a3[g] * carry + b3[g]
            if dst_ref is not None:
                dst_ref[pl.ds(base + g * 8, 8), :] = h
            carry = h[7:8] if d == 0 else h[0:1]
        return carry

    def body(c, carry):
        cf, cbk = carry
        a, bb = gates(conv(c, 0), 0)
        cf = scan(a, bb, cf, c, 0, hf_ref if with_out else None)
        cr = n_chunks - 1 - c
        a, bb = gates(conv(cr, 1), 1)
        cbk = scan(a, bb, cbk, cr, 1, hb_ref if with_out else None)
        return cf, cbk

    h0 = h0_ref[...]
    cf, cbk = lax.fori_loop(0, n_chunks, body, (h0[0:1], h0[1:2]))
    hfin_ref[...] = jnp.concatenate([cf, cbk], axis=0)

    if with_out:
        def finish(c, _):
            base = pl.multiple_of(c * chunk, chunk)
            s = s_ref[pl.ds(base, chunk), :].astype(F32)
            y = hf_ref[pl.ds(base, chunk), :] + hb_ref[pl.ds(base, chunk), :]
            y_ref[pl.ds(base, chunk), :] = (y * _silu(s)).astype(y_ref.dtype)
            return 0
        lax.fori_loop(0, n_chunks, finish, 0)


def _lru_mixer(z, s_col_blk, wg, prm, h0, width, with_out):
    b, l, _ = z.shape
    n_cb, _, cb, _ = wg.shape
    n_conv = 4
    chunk = _row_tile(l, 256)
    kern = functools.partial(_lru_kernel, seq=l, chunk=chunk, n_conv=n_conv, with_out=with_out)
    x_spec = pl.BlockSpec((None, l, cb), lambda bi, ci: (bi, 0, ci))
    s_spec = pl.BlockSpec((None, l, cb), lambda bi, ci: (bi, 0, s_col_blk + ci))
    wg_spec = pl.BlockSpec((None, 2, cb, 2 * cb), lambda bi, ci: (ci, 0, 0, 0))
    prm_spec = pl.BlockSpec((2, 8, cb), lambda bi, ci: (0, 0, ci))
    st_spec = pl.BlockSpec((None, 2, cb), lambda bi, ci: (bi, 0, ci))
    st_shape = jax.ShapeDtypeStruct((b, 2, width), F32)
    if with_out:
        y, hfin = pl.pallas_call(
            kern, grid=(b, n_cb),
            in_specs=[x_spec, s_spec, wg_spec, prm_spec, st_spec],
            out_specs=[pl.BlockSpec((None, l, cb), lambda bi, ci: (bi, 0, ci)), st_spec],
            out_shape=[jax.ShapeDtypeStruct((b, l, width), BF16), st_shape],
            scratch_shapes=[pltpu.VMEM((l, cb), F32), pltpu.VMEM((l, cb), F32)],
            compiler_params=_params("parallel", "parallel"),
            name="rglru_mixer",
        )(z, z, wg, prm, h0)
        return y, hfin
    hfin = pl.pallas_call(
        kern, grid=(b, n_cb),
        in_specs=[x_spec, wg_spec, prm_spec, st_spec],
        out_specs=st_spec,
        out_shape=st_shape,
        compiler_params=_params("parallel", "parallel"),
        name="rglru_state",
    )(z, wg, prm, h0)
    return None, hfin


_CONV_ROWS = 64
_CONV_SLAB = 96
_EPI_ROWS = 64
_LANES = 128


def _halo_specs(t, width, col_blk, n_row_blocks):
    r = t // BF16_SUBLANES
    last = n_row_blocks * r - 1
    cur = pl.BlockSpec((None, t, width), lambda bi, i: (bi, i, col_blk))
    prev = pl.BlockSpec((None, BF16_SUBLANES, width), lambda bi, i: (bi, jnp.maximum(i * r - 1, 0), col_blk))
    nxt = pl.BlockSpec((None, BF16_SUBLANES, width), lambda bi, i: (bi, jnp.minimum((i + 1) * r, last), col_blk))
    return prev, cur, nxt


def _fill_window(win_ref, fn, prev_refs, cur_refs, next_refs, t):
    i = pl.program_id(1)
    n = pl.num_programs(1)
    h = BF16_SUBLANES
    load = lambda refs: [r[...].astype(F32) for r in refs]
    win_ref[0:h] = fn(*load(prev_refs)) * jnp.where(i > 0, 1.0, 0.0)
    win_ref[h:h + t] = fn(*load(cur_refs))
    win_ref[h + t:h + t + h] = fn(*load(next_refs)) * jnp.where(i < n - 1, 1.0, 0.0)


def _dw_conv(win_ref, wb_ref, dst_ref, t, n_taps):
    off = BF16_SUBLANES - n_taps // 2
    assert off >= 0 and off + n_taps - 1 + _CONV_ROWS <= _CONV_SLAB
    width = dst_ref.shape[-1]
    groups = _CONV_ROWS // 8

    def tile(ti, _):
        r0 = pl.multiple_of(ti * _CONV_ROWS, _CONV_ROWS)
        for c0 in range(0, width, _LANES):
            slab = win_ref[pl.ds(r0, _CONV_SLAB), c0:c0 + _LANES]
            acc = None
            for res in range(8):
                taps = [o for o in range(off, off + n_taps) if o % 8 == res]
                if not taps:
                    continue
                sh = slab if res == 0 else pltpu.roll(slab, _CONV_SLAB - res, axis=0)
                for o in taps:
                    q = 8 * (o // 8)
                    term = sh[q:q + _CONV_ROWS].reshape(groups, 8, _LANES) * wb_ref[o - off][:, c0:c0 + _LANES]
                    acc = term if acc is None else acc + term
            dst_ref[pl.ds(r0, _CONV_ROWS), c0:c0 + _LANES] = acc.reshape(_CONV_ROWS, _LANES)
        return 0

    lax.fori_loop(0, t // _CONV_ROWS, tile, 0)


def _conformer_kernel(vp, gp, vc, gc, vn, gn, s_ref, wb_ref, prm_ref, o_ref, win_ref, conv_ref, *, t, n_taps):
    _fill_window(win_ref, lambda v, g: v * _sigmoid(g), (vp, gp), (vc, gc), (vn, gn), t)
    _dw_conv(win_ref, wb_ref, conv_ref, t, n_taps)

    def tile(ti, _):
        r0 = pl.multiple_of(ti * _EPI_ROWS, _EPI_ROWS)
        acc = conv_ref[pl.ds(r0, _EPI_ROWS), :] + prm_ref[0:1, :]
        mu = jnp.mean(acc, axis=-1, keepdims=True)
        d = acc - mu
        var = jnp.mean(d * d, axis=-1, keepdims=True)
        yn = d * lax.rsqrt(var + EPS) * prm_ref[1:2, :] + prm_ref[2:3, :]
        s = s_ref[pl.ds(r0, _EPI_ROWS), :].astype(F32)
        o_ref[pl.ds(r0, _EPI_ROWS), :] = (_silu(yn) * _silu(s)).astype(o_ref.dtype)
        return 0

    trips = t // _EPI_ROWS
    lax.fori_loop(0, trips, tile, 0, unroll=2 if trips % 2 == 0 else 1)


def _shortconv_kernel(xp, cp, xc, cc, xn, cn, gb_ref, s_ref, wb_ref, o_ref, win_ref, conv_ref, *, t, n_taps):
    _fill_window(win_ref, lambda x, c: c * x, (xp, cp), (xc, cc), (xn, cn), t)
    _dw_conv(win_ref, wb_ref, conv_ref, t, n_taps)

    def tile(ti, _):
        r0 = pl.multiple_of(ti * _EPI_ROWS, _EPI_ROWS)
        gb = gb_ref[pl.ds(r0, _EPI_ROWS), :].astype(F32)
        s = s_ref[pl.ds(r0, _EPI_ROWS), :].astype(F32)
        o_ref[pl.ds(r0, _EPI_ROWS), :] = (gb * conv_ref[pl.ds(r0, _EPI_ROWS), :] * _silu(s)).astype(o_ref.dtype)
        return 0

    trips = t // _EPI_ROWS
    lax.fori_loop(0, trips, tile, 0, unroll=2 if trips % 2 == 0 else 1)


def _sublane_repeat(w):
    return jnp.broadcast_to(w.astype(F32)[:, None, :], (w.shape[0], 8, w.shape[1]))


def _conv_scratch(t, width):
    return [pltpu.VMEM((t + 2 * BF16_SUBLANES, width), F32), pltpu.VMEM((t, width), F32)]


def _conformer_mixer(z, w_conv, prm, width, col_val, col_gl, col_s):
    b, l, _ = z.shape
    t = _row_tile(l, 512)
    nb = l // t
    n_taps = w_conv.shape[0]
    wb = _sublane_repeat(w_conv)
    vp, vc, vn = _halo_specs(t, width, col_val, nb)
    gp, gc, gn = _halo_specs(t, width, col_gl, nb)
    return pl.pallas_call(
        functools.partial(_conformer_kernel, t=t, n_taps=n_taps),
        grid=(b, nb),
        in_specs=[vp, gp, vc, gc, vn, gn,
                  pl.BlockSpec((None, t, width), lambda bi, i: (bi, i, col_s)),
                  pl.BlockSpec(wb.shape, lambda bi, i: (0, 0, 0)),
                  pl.BlockSpec(prm.shape, lambda bi, i: (0, 0))],
        out_specs=pl.BlockSpec((None, t, width), lambda bi, i: (bi, i, 0)),
        out_shape=jax.ShapeDtypeStruct((b, l, width), BF16),
        scratch_shapes=_conv_scratch(t, width),
        compiler_params=_params("parallel", "parallel"),
        name="conformer_conv",
    )(z, z, z, z, z, z, z, wb, prm)


def _shortconv_mixer(z, w_conv, width, col_x, col_gb, col_gc, col_s):
    b, l, _ = z.shape
    t = _row_tile(l, 512)
    nb = l // t
    n_taps = w_conv.shape[0]
    wb = _sublane_repeat(w_conv)
    xp, xc, xn = _halo_specs(t, width, col_x, nb)
    cp, cc, cn = _halo_specs(t, width, col_gc, nb)
    return pl.pallas_call(
        functools.partial(_shortconv_kernel, t=t, n_taps=n_taps),
        grid=(b, nb),
        in_specs=[xp, cp, xc, cc, xn, cn,
                  pl.BlockSpec((None, t, width), lambda bi, i: (bi, i, col_gb)),
                  pl.BlockSpec((None, t, width), lambda bi, i: (bi, i, col_s)),
                  pl.BlockSpec(wb.shape, lambda bi, i: (0, 0, 0))],
        out_specs=pl.BlockSpec((None, t, width), lambda bi, i: (bi, i, 0)),
        out_shape=jax.ShapeDtypeStruct((b, l, width), BF16),
        scratch_shapes=_conv_scratch(t, width),
        compiler_params=_params("parallel", "parallel"),
        name="short_conv",
    )(z, z, z, z, z, z, z, z, wb)


def _dft_split(n):
    n2 = 1 << (int(math.log2(n)) // 2)
    n1 = n // n2
    assert n1 * n2 == n
    return n1, n2


def _cos_sin(rows, cols, period):
    ang = 2.0 * np.pi * ((np.outer(np.arange(rows), np.arange(cols))) % period) / period
    return np.cos(ang), np.sin(ang)


def _fnet_tables(seq, group_width):
    n1, n2 = _dft_split(seq)
    cc, sc = _cos_sin(group_width, group_width, group_width)
    c1, s1 = _cos_sin(n1, n1, n1)
    c2, s2 = _cos_sin(n2, n2, n2)
    tc, ts = _cos_sin(n2, n1, seq)
    lanes = 128
    tables = dict(
        cs=np.concatenate([cc, sc], axis=1),
        m1=np.block([[c1, -s1], [s1, c1]]),
        m2=np.concatenate([c2, -s2], axis=1),
        tc=np.broadcast_to(tc[:, :, None], (n2, n1, lanes)),
        ts=np.broadcast_to(ts[:, :, None], (n2, n1, lanes)),
    )
    out = {k: jnp.asarray(np.ascontiguousarray(v), F32) for k, v in tables.items()}
    for k in ("cs", "m1", "m2"):
        out[k] = out[k].astype(BF16)
    return out


def _fnet_stage1_kernel(x_ref, cs_ref, m1_ref, tc_ref, ts_ref, or_ref, oi_ref, *, groups):
    n1, nb, width = x_ref.shape
    gw = width // groups
    x = pltpu.einshape("abc->bac", x_ref[...]).reshape(nb * n1, width)
    ps, qs = [], []
    for g in range(groups):
        pq = jnp.dot(x[:, g * gw:(g + 1) * gw], cs_ref[...], preferred_element_type=F32)
        ps.append(pq[:, :gw])
        qs.append(pq[:, gw:])
    p = jnp.concatenate(ps, axis=1).astype(BF16)
    q = jnp.concatenate(qs, axis=1).astype(BF16)
    reps = width // tc_ref.shape[-1]
    res_r, res_i = [], []
    for j in range(nb):
        pq = jnp.concatenate([p[j * n1:(j + 1) * n1], q[j * n1:(j + 1) * n1]], axis=0)
        a = jnp.dot(m1_ref[...], pq, preferred_element_type=F32)
        ar, ai = a[:n1], a[n1:]
        tc = jnp.tile(tc_ref[j], (1, reps))
        ts = jnp.tile(ts_ref[j], (1, reps))
        res_r.append((ar * tc - ai * ts).astype(or_ref.dtype))
        res_i.append((ar * ts + ai * tc).astype(oi_ref.dtype))
    or_ref[...] = pltpu.einshape("abc->bac", jnp.stack(res_r, axis=0))
    oi_ref[...] = pltpu.einshape("abc->bac", jnp.stack(res_i, axis=0))


def _fnet_stage2_kernel(ar_ref, ai_ref, m2_ref, s_ref, o_ref, *, scale):
    kb = ar_ref.shape[0]
    res = []
    for j in range(kb):
        rhs = jnp.concatenate([ar_ref[j], ai_ref[j]], axis=0)
        res.append(jnp.dot(m2_ref[...], rhs, preferred_element_type=F32) * scale)
    f = pltpu.einshape("abc->bac", jnp.stack(res, axis=0))
    o_ref[...] = (f * _silu(s_ref[...].astype(F32))).astype(o_ref.dtype)


def _fnet_mixer(zd, z, col_s, tables):
    b, l, width = zd.shape
    n1, n2 = _dft_split(l)
    gw = width // FNET_GROUPS
    nb = min(n2, BF16_SUBLANES)
    kb = min(n1, BF16_SUBLANES)
    x4 = zd.reshape(b, n1, n2, width)
    blk = pl.BlockSpec((None, n1, nb, width), lambda bi, j: (bi, 0, j, 0))
    tw = pl.BlockSpec((nb, n1, 128), lambda bi, j: (j, 0, 0))
    full = lambda a: pl.BlockSpec(a.shape, lambda bi, j: (0,) * a.ndim)
    ar, ai = pl.pallas_call(
        functools.partial(_fnet_stage1_kernel, groups=FNET_GROUPS),
        grid=(b, n2 // nb),
        in_specs=[blk, full(tables["cs"]), full(tables["m1"]), tw, tw],
        out_specs=[blk, blk],
        out_shape=[jax.ShapeDtypeStruct((b, n1, n2, width), BF16)] * 2,
        compiler_params=_params("parallel", "parallel"),
        name="fnet_stage1",
    )(x4, tables["cs"], tables["m1"], tables["tc"], tables["ts"])
    blk2 = pl.BlockSpec((None, kb, n2, width), lambda bi, j: (bi, j, 0, 0))
    z4 = z.reshape(b, n2, n1, z.shape[-1])
    scale = 1.0 / math.sqrt(l * gw)
    f = pl.pallas_call(
        functools.partial(_fnet_stage2_kernel, scale=scale),
        grid=(b, n1 // kb),
        in_specs=[blk2, blk2, full(tables["m2"]),
                  pl.BlockSpec((None, n2, kb, width), lambda bi, j: (bi, 0, j, col_s))],
        out_specs=pl.BlockSpec((None, n2, kb, width), lambda bi, j: (bi, 0, j, 0)),
        out_shape=jax.ShapeDtypeStruct((b, n2, n1, width), BF16),
        compiler_params=_params("parallel", "parallel"),
        name="fnet_stage2",
    )(ar, ai, tables["m2"], z4)
    return f.reshape(b, l, width)


def _branch_kernel(pa, pb, pc, pd, w_ref, ga, gb, gc, gd, o_ref):
    acc = None
    for k, (p, g) in enumerate(((pa, ga), (pb, gb), (pc, gc), (pd, gd))):
        br = jnp.dot(p[...], w_ref[k], preferred_element_type=F32)
        term = _sigmoid(g[...].astype(F32)) * br
        acc = term if acc is None else acc + term
    o_ref[...] = acc.astype(o_ref.dtype)


def _branch_merge(ps, w_branch, layer, z2d, gate_col0, d_model):
    m, width = ps[0].shape
    tm = _row_tile(m, 1024)
    tn = _row_tile(d_model, 512)
    assert gate_col0 % tn == 0
    g0 = gate_col0 // tn
    per = d_model // tn
    p_spec = pl.BlockSpec((tm, width), lambda i, j: (i, 0))
    g_specs = [pl.BlockSpec((tm, tn), functools.partial(lambda i, j, k: (i, g0 + k * per + j), k=k))
               for k in range(N_BRANCH)]
    return pl.pallas_call(
        _branch_kernel,
        grid=(m // tm, d_model // tn),
        in_specs=[p_spec] * 4 + [pl.BlockSpec((None, N_BRANCH, width, tn), lambda i, j: (layer, 0, 0, j))] + g_specs,
        out_specs=pl.BlockSpec((tm, tn), lambda i, j: (i, j)),
        out_shape=jax.ShapeDtypeStruct((m, d_model), BF16),
        compiler_params=_params("parallel", "parallel"),
        name="branch_merge",
    )(*ps, w_branch, z2d, z2d, z2d, z2d)


def _out_norm_kernel(*refs, final):
    if final:
        m_ref, w_ref, x_ref, gate_ref, g_ref, h_ref, xrow_ref = refs
        o_ref = sc_ref = sh_ref = None
    else:
        m_ref, w_ref, x_ref, gate_ref, g_ref, sc_ref, sh_ref, o_ref, h_ref, xrow_ref = refs
    j = pl.program_id(2)
    nj = xrow_ref.shape[0] + 1
    tn = x_ref.shape[-1]

    def residual():
        xn = x_ref[...] + gate_ref[...] * jnp.dot(m_ref[...], w_ref[...], preferred_element_type=F32)
        if o_ref is not None:
            o_ref[...] = xn
        return xn

    @pl.when(j < nj - 1)
    def _():
        xrow_ref[j] = residual()

    @pl.when(j == nj - 1)
    def _():
        parts = [xrow_ref[jj] for jj in range(nj - 1)] + [residual()]
        ss = parts[0] * parts[0]
        for p in parts[1:]:
            ss = ss + p * p
        rstd = lax.rsqrt(jnp.sum(ss, axis=-1, keepdims=True) / (nj * tn) + EPS)
        for jj, p in enumerate(parts):
            sl = slice(jj * tn, (jj + 1) * tn)
            n = p * rstd * g_ref[:, sl]
            if not final:
                n = n * (1.0 + sc_ref[:, sl]) + sh_ref[:, sl]
            h_ref[:, sl] = n.astype(h_ref.dtype)


def _out_proj_residual_norm(merged, w_out, layer, x, gate, g_next, scale_next=None, shift_next=None):
    b, l, d = x.shape
    final = scale_next is None
    tm = _row_tile(l, 512)
    tn = _row_tile(d, 1024)
    nj = d // tn
    assert nj >= 2
    row = lambda bi, i, j: (bi, i, 0)
    vec = lambda bi, i, j: (bi, 0, 0)
    tile = lambda bi, i, j: (bi, i, j)
    in_specs = [pl.BlockSpec((None, tm, d), row),
                pl.BlockSpec((None, d, tn), lambda bi, i, j: (layer, 0, j)),
                pl.BlockSpec((None, tm, tn), tile),
                pl.BlockSpec((None, 1, tn), lambda bi, i, j: (bi, 0, j)),
                pl.BlockSpec((1, d), lambda bi, i, j: (0, 0))]
    args = [merged, w_out, x, gate, g_next]
    if final:
        out_specs = pl.BlockSpec((None, tm, d), row)
        out_shape = jax.ShapeDtypeStruct((b, l, d), x.dtype)
    else:
        in_specs += [pl.BlockSpec((None, 1, d), vec), pl.BlockSpec((None, 1, d), vec)]
        args += [scale_next, shift_next]
        out_specs = [pl.BlockSpec((None, tm, tn), tile), pl.BlockSpec((None, tm, d), row)]
        out_shape = [jax.ShapeDtypeStruct((b, l, d), x.dtype), jax.ShapeDtypeStruct((b, l, d), BF16)]
    return pl.pallas_call(
        functools.partial(_out_norm_kernel, final=final),
        grid=(b, l // tm, nj),
        in_specs=in_specs,
        out_specs=out_specs,
        out_shape=out_shape,
        scratch_shapes=[pltpu.VMEM((nj - 1, tm, tn), F32)],
        compiler_params=_params("parallel", "parallel", "arbitrary"),
        name="out_proj_final_norm" if final else "out_proj_residual_norm",
    )(*args)


def _block_diag_gates(wa, wx, cb):
    depth, _, heads, hd, _ = wa.shape
    hpb = cb // hd
    n_cb = heads // hpb
    eye = jnp.eye(hpb, dtype=wa.dtype)

    def bd(w):
        w = w.reshape(depth, 2, n_cb, hpb, hd, hd)
        full = jnp.einsum("ldchij,hg->ldchigj", w, eye)
        return full.reshape(depth, 2, n_cb, cb, cb)

    both = 0.5 * jnp.concatenate([bd(wa), bd(wx)], axis=-1)
    return both.transpose(0, 2, 1, 3, 4).astype(BF16)


def _token_mix(z, zd, y_a, width, d_model, lp, tables):
    b, l, n_main = z.shape
    y_b = _conformer_mixer(z, lp["cf_conv_w"], lp["cf_prm"], width, 1, 2, 7)
    y_c = _shortconv_mixer(z, lp["sc_conv_w"], width, 3, 4, 5, 8)
    y_d = _fnet_mixer(zd, z, 9, tables)
    m = b * l
    ps = [y.reshape(m, width) for y in (y_a, y_b, y_c, y_d)]
    merged = _branch_merge(ps, lp["w_branch"], lp["layer"], z.reshape(m, n_main), 10 * width, d_model)
    return merged.reshape(b, l, d_model)


def kernel(x, c, ctx, c_ctx, norm_g, w_mod, b_mod, w_in, lru_conv_w, lru_conv_b, lru_wa, lru_ba, lru_wx,
           lru_bx, lru_lambda, cf_conv_w, cf_conv_b, cf_ln_g, cf_ln_b, sc_conv_w, w_branch, w_out, final_g):
    bsz, seq, d_model = x.shape
    ctx_len = ctx.shape[1]
    depth = w_in.shape[0]
    width = d_model // N_BRANCH
    assert bsz + 1 <= 8

    four_blk = 6
    n_main_blk = w_in.shape[-1] // width - 1
    n_main = n_main_blk * width
    w_branch_b = w_branch.astype(BF16)
    w_out_b = w_out.astype(BF16)

    cb = min(256, width)
    wg = _block_diag_gates(lru_wa, lru_wx, cb)
    lru_prm = jnp.concatenate([lru_conv_w, lru_conv_b[:, :, None], 0.5 * lru_ba[:, :, None],
                               0.5 * lru_bx[:, :, None], lru_lambda[:, :, None]], axis=2).astype(F32)
    cf_prm = jnp.pad(jnp.stack([cf_conv_b, cf_ln_g, cf_ln_b], axis=1), ((0, 0), (0, 5), (0, 0)))
    tables_lat = _fnet_tables(seq, width // FNET_GROUPS)
    tables_ctx = _fnet_tables(ctx_len, width // FNET_GROUPS)

    cin = jnp.concatenate([c, c_ctx[None], jnp.zeros((8 - bsz - 1, d_model), c.dtype)], axis=0)
    mods = _adaln_mod(cin, w_mod, b_mod)

    xc = ctx
    s_blk = (6 * width) // cb
    rows_c = jnp.full((bsz,), bsz, jnp.int32)

    def mod_parts(l):
        shift, scale, gate = (mods[l, :, k * d_model:(k + 1) * d_model] for k in range(3))
        return dict(g=norm_g[l][None], gate=gate[:bsz, None], gate_c=gate[rows_c][:, None],
                    scale=scale[:bsz, None], shift=shift[:bsz, None],
                    scale_c=scale[rows_c][:, None], shift_c=shift[rows_c][:, None])

    mp = mod_parts(0)
    hc = _norm_mod(xc, mp["g"], mp["scale_c"], mp["shift_c"])
    h = _norm_mod(x, mp["g"], mp["scale"], mp["shift"])
    for l in range(depth):
        last = l == depth - 1
        lp = dict(cf_conv_w=cf_conv_w[l], cf_prm=cf_prm[l], sc_conv_w=sc_conv_w[l], w_branch=w_branch_b, layer=l)

        hc2 = hc.reshape(bsz * ctx_len, d_model)
        zc = _in_proj(hc2, w_in, l, 0, 1 if last else n_main_blk, four_blk, width, "in_proj_ctx")
        zc = zc.reshape(bsz, ctx_len, -1)
        h0 = jnp.zeros((bsz, 2, width), F32)
        yc_a, h_ctx = _lru_mixer(zc, s_blk, wg[l], lru_prm[l], h0, width, with_out=not last)

        h2 = h.reshape(bsz * seq, d_model)
        z = _in_proj(h2, w_in, l, 0, n_main_blk, four_blk, width, "in_proj").reshape(bsz, seq, n_main)
        zd = _in_proj(h2, w_in, l, four_blk, 1, None, width, "in_proj_fourier").reshape(bsz, seq, width)
        y_a, _ = _lru_mixer(z, s_blk, wg[l], lru_prm[l], h_ctx, width, with_out=True)
        merged = _token_mix(z, zd, y_a, width, d_model, lp, tables_lat)
        if last:
            return _out_proj_residual_norm(merged, w_out_b, l, x, mp["gate"], final_g[None])
        else:
            zcd = _in_proj(hc2, w_in, l, four_blk, 1, None, width, "in_proj_fourier_ctx")
            zcd = zcd.reshape(bsz, ctx_len, width)
            merged_c = _token_mix(zc, zcd, yc_a, width, d_model, lp, tables_ctx)
            nxt = mod_parts(l + 1)
            xc, hc = _out_proj_residual_norm(merged_c, w_out_b, l, xc, mp["gate_c"],
                                             nxt["g"], nxt["scale_c"], nxt["shift_c"])
            x, h = _out_proj_residual_norm(merged, w_out_b, l, x, mp["gate"],
                                           nxt["g"], nxt["scale"], nxt["shift"])
            mp = nxt
```

```python
import functools
import math

import numpy as np
import jax
import jax.numpy as jnp
from jax import lax
from jax.experimental import pallas as pl
from jax.experimental.pallas import tpu as pltpu

EPS = 1e-6
LRU_C = 8.0
N_BRANCH = 4
FNET_GROUPS = 4
V7X_VMEM_LIMIT_BYTES = 56 * 1024 * 1024
BF16_SUBLANES = 16
F32 = jnp.float32
BF16 = jnp.bfloat16


def _params(*sem):
    return pltpu.CompilerParams(dimension_semantics=sem, vmem_limit_bytes=V7X_VMEM_LIMIT_BYTES)


def _sigmoid(v):
    return 0.5 * jnp.tanh(0.5 * v) + 0.5


def _silu(v):
    return v * _sigmoid(v)


def _mod_kernel(c_ref, w_ref, b_ref, o_ref):
    s = _silu(c_ref[...]).astype(BF16)
    w = w_ref[0].astype(BF16)
    o_ref[0] = jnp.dot(s, w, preferred_element_type=F32) + b_ref[0]


def _adaln_mod(cin, w_mod, b_mod):
    depth, d, n = w_mod.shape
    tn = min(512, n)
    return pl.pallas_call(
        _mod_kernel,
        grid=(depth, n // tn),
        in_specs=[pl.BlockSpec((8, d), lambda l, j: (0, 0)),
                  pl.BlockSpec((1, d, tn), lambda l, j: (l, 0, j)),
                  pl.BlockSpec((1, 1, tn), lambda l, j: (l, 0, j))],
        out_specs=pl.BlockSpec((1, 8, tn), lambda l, j: (l, 0, j)),
        out_shape=jax.ShapeDtypeStruct((depth, 8, n), F32),
        compiler_params=_params("parallel", "parallel"),
        name="adaln_mod",
    )(cin, w_mod, b_mod.reshape(depth, 1, n))


def _norm_mod_kernel(x_ref, g_ref, sc_ref, sh_ref, o_ref):
    x = x_ref[...]
    n = x * lax.rsqrt(jnp.mean(x * x, axis=-1, keepdims=True) + EPS) * g_ref[...]
    o_ref[...] = (n * (1.0 + sc_ref[...]) + sh_ref[...]).astype(o_ref.dtype)


def _row_tile(l, cap):
    t = min(l, cap)
    while l % t:
        t //= 2
    return t


def _norm_mod(x, g, scale, shift):
    b, l, d = x.shape
    tr = _row_tile(l, 256)
    return pl.pallas_call(
        _norm_mod_kernel,
        grid=(b, l // tr),
        in_specs=[pl.BlockSpec((None, tr, d), lambda bi, i: (bi, i, 0)),
                  pl.BlockSpec((1, d), lambda bi, i: (0, 0)),
                  pl.BlockSpec((None, 1, d), lambda bi, i: (bi, 0, 0)),
                  pl.BlockSpec((None, 1, d), lambda bi, i: (bi, 0, 0))],
        out_specs=pl.BlockSpec((None, tr, d), lambda bi, i: (bi, i, 0)),
        out_shape=jax.ShapeDtypeStruct((b, l, d), BF16),
        compiler_params=_params("parallel", "parallel"),
        name="norm_mod",
    )(x, g, scale, shift)


def _in_proj_kernel(h_ref, w_hbm, o_ref, wf_ref, wb_ref, sem, *, layer, first_blk, skip_blk):
    j = pl.program_id(0)
    i = pl.program_id(1)
    tn = wb_ref.shape[-1]
    n_slots = wf_ref.shape[0]

    def panel_copy(jj):
        blk = first_blk + jj
        if skip_blk is not None:
            blk = blk + jnp.where(blk >= skip_blk, 1, 0)
        col = pl.multiple_of(blk * tn, tn)
        slot = jj % n_slots
        return pltpu.make_async_copy(w_hbm.at[layer, :, pl.ds(col, tn)], wf_ref.at[slot], sem.at[slot])

    def start_next():
        @pl.when(j + 1 < pl.num_programs(0))
        def _():
            panel_copy(j + 1).start()

    @pl.when(i == 0)
    def _():
        @pl.when(j == 0)
        def _():
            panel_copy(j).start()

        if n_slots > 1:
            start_next()
        panel_copy(j).wait()
        wb_ref[...] = wf_ref[j % n_slots].astype(BF16)
        if n_slots == 1:
            start_next()

    o_ref[...] = jnp.dot(h_ref[...], wb_ref[...], preferred_element_type=F32).astype(o_ref.dtype)


def _in_proj(h, w_in, layer, first_blk, n_blk, skip_blk, tn, name):
    m, k = h.shape
    tm = _row_tile(m, 1024)
    n_slots = 2 if m == tm else 1
    return pl.pallas_call(
        functools.partial(_in_proj_kernel, layer=layer, first_blk=first_blk, skip_blk=skip_blk),
        grid=(n_blk, m // tm),
        in_specs=[pl.BlockSpec((tm, k), lambda j, i: (i, 0)),
                  pl.BlockSpec(memory_space=pl.ANY)],
        out_specs=pl.BlockSpec((tm, tn), lambda j, i: (i, j)),
        out_shape=jax.ShapeDtypeStruct((m, n_blk * tn), BF16),
        scratch_shapes=[pltpu.VMEM((n_slots, k, tn), F32), pltpu.VMEM((k, tn), BF16),
                        pltpu.SemaphoreType.DMA((n_slots,))],
        compiler_params=_params("arbitrary", "arbitrary"),
        name=name,
    )(h, w_in)


_LRU_ROW_CONV_B, _LRU_ROW_BA, _LRU_ROW_BX, _LRU_ROW_LAMBDA = 4, 5, 6, 7


def _lru_kernel(*refs, seq, chunk, n_conv, with_out):
    if with_out:
        x_ref, s_ref, wg_ref, prm_ref, h0_ref, y_ref, hfin_ref, hf_ref, hb_ref = refs
    else:
        x_ref, wg_ref, prm_ref, h0_ref, hfin_ref = refs
    cb = x_ref.shape[-1]
    n_chunks = seq // chunk
    groups = chunk // 8
    halo = BF16_SUBLANES
    row = lax.broadcasted_iota(jnp.int32, (groups, 8, cb), 1)

    def gates(u, d):
        prm = prm_ref[d]
        g = jnp.dot(u.astype(BF16), wg_ref[d], preferred_element_type=F32)
        t_r = jnp.tanh(g[:, :cb] + prm[_LRU_ROW_BA:_LRU_ROW_BA + 1])
        t_i = jnp.tanh(g[:, cb:] + prm[_LRU_ROW_BX:_LRU_ROW_BX + 1])
        lam = prm[_LRU_ROW_LAMBDA:_LRU_ROW_LAMBDA + 1]
        c_half = (0.5 * LRU_C) * (jnp.minimum(lam, 0.0) - jnp.log1p(jnp.exp(-jnp.abs(lam))))
        log_a = c_half * t_r + c_half
        a = jnp.exp(log_a)
        th = jnp.tanh(log_a)
        q = (-0.5 * th) / (1.0 - th)
        bb = jnp.sqrt(q) * (u * (t_i + 1.0))
        return a, bb

    def conv(c, d):
        base = pl.multiple_of(c * chunk, chunk)
        cur = x_ref[pl.ds(base, chunk), :].astype(F32)
        prm = prm_ref[d]
        if d == 0:
            pbase = pl.multiple_of(jnp.maximum(base - halo, 0), halo)
            edge = x_ref[pl.ds(pbase, halo), :].astype(F32) * jnp.where(c > 0, 1.0, 0.0)
            win = jnp.concatenate([edge, cur], axis=0)
            off = halo - (n_conv - 1)
        else:
            nbase = pl.multiple_of(jnp.minimum(base + chunk, seq - halo), halo)
            edge = x_ref[pl.ds(nbase, halo), :].astype(F32) * jnp.where(c < n_chunks - 1, 1.0, 0.0)
            win = jnp.concatenate([cur, edge], axis=0)
            off = 0
        u = jnp.broadcast_to(prm[_LRU_ROW_CONV_B:_LRU_ROW_CONV_B + 1], (chunk, cb))
        for k in range(n_conv):
            u = u + win[off + k:off + k + chunk] * prm[k:k + 1]
        return u

    def scan(a, bb, carry, c, d, dst_ref):
        a3 = a.reshape(groups, 8, cb)
        b3 = bb.reshape(groups, 8, cb)
        for s in (1, 2, 4):
            shift = s if d == 0 else 8 - s
            valid = (row >= s) if d == 0 else (row < 8 - s)
            a_sh = jnp.where(valid, pltpu.roll(a3, shift, axis=1), 1.0)
            b_sh = jnp.where(valid, pltpu.roll(b3, shift, axis=1), 0.0)
            b3 = a3 * b_sh + b3
            a3 = a3 * a_sh
        base = pl.multiple_of(c * chunk, chunk)
        order = range(groups) if d == 0 else range(groups - 1, -1, -1)
        for g in order:
            h = a3[g] * carry + b3[g]
            if dst_ref is not None:
                dst_ref[pl.ds(base + g * 8, 8), :] = h
            carry = h[7:8] if d == 0 else h[0:1]
        return carry

    def body(c, carry):
        cf, cbk = carry
        a, bb = gates(conv(c, 0), 0)
        cf = scan(a, bb, cf, c, 0, hf_ref if with_out else None)
        cr = n_chunks - 1 - c
        a, bb = gates(conv(cr, 1), 1)
        cbk = scan(a, bb, cbk, cr, 1, hb_ref if with_out else None)
        return cf, cbk

    h0 = h0_ref[...]
    cf, cbk = lax.fori_loop(0, n_chunks, body, (h0[0:1], h0[1:2]))
    hfin_ref[...] = jnp.concatenate([cf, cbk], axis=0)

    if with_out:
        def finish(c, _):
            base = pl.multiple_of(c * chunk, chunk)
            s = s_ref[pl.ds(base, chunk), :].astype(F32)
            y = hf_ref[pl.ds(base, chunk), :] + hb_ref[pl.ds(base, chunk), :]
            y_ref[pl.ds(base, chunk), :] = (y * _silu(s)).astype(y_ref.dtype)
            return 0
        lax.fori_loop(0, n_chunks, finish, 0)


def _lru_mixer(z, s_col_blk, wg, prm, h0, width, with_out):
    b, l, _ = z.shape
    n_cb, _, cb, _ = wg.shape
    n_conv = 4
    chunk = _row_tile(l, 256)
    kern = functools.partial(_lru_kernel, seq=l, chunk=chunk, n_conv=n_conv, with_out=with_out)
    x_spec = pl.BlockSpec((None, l, cb), lambda bi, ci: (bi, 0, ci))
    s_spec = pl.BlockSpec((None, l, cb), lambda bi, ci: (bi, 0, s_col_blk + ci))
    wg_spec = pl.BlockSpec((None, 2, cb, 2 * cb), lambda bi, ci: (ci, 0, 0, 0))
    prm_spec = pl.BlockSpec((2, 8, cb), lambda bi, ci: (0, 0, ci))
    st_spec = pl.BlockSpec((None, 2, cb), lambda bi, ci: (bi, 0, ci))
    st_shape = jax.ShapeDtypeStruct((b, 2, width), F32)
    if with_out:
        y, hfin = pl.pallas_call(
            kern, grid=(b, n_cb),
            in_specs=[x_spec, s_spec, wg_spec, prm_spec, st_spec],
            out_specs=[pl.BlockSpec((None, l, cb), lambda bi, ci: (bi, 0, ci)), st_spec],
            out_shape=[jax.ShapeDtypeStruct((b, l, width), BF16), st_shape],
            scratch_shapes=[pltpu.VMEM((l, cb), F32), pltpu.VMEM((l, cb), F32)],
            compiler_params=_params("parallel", "parallel"),
            name="rglru_mixer",
        )(z, z, wg, prm, h0)
        return y, hfin
    hfin = pl.pallas_call(
        kern, grid=(b, n_cb),
        in_specs=[x_spec, wg_spec, prm_spec, st_spec],
        out_specs=st_spec,
        out_shape=st_shape,
        compiler_params=_params("parallel", "parallel"),
        name="rglru_state",
    )(z, wg, prm, h0)
    return None, hfin


_CONV_ROWS = 64
_CONV_SLAB = 96
_EPI_ROWS = 64
_LANES = 128


def _halo_specs(t, width, col_blk, n_row_blocks):
    r = t // BF16_SUBLANES
    last = n_row_blocks * r - 1
    cur = pl.BlockSpec((None, t, width), lambda bi, i: (bi, i, col_blk))
    prev = pl.BlockSpec((None, BF16_SUBLANES, width), lambda bi, i: (bi, jnp.maximum(i * r - 1, 0), col_blk))
    nxt = pl.BlockSpec((None, BF16_SUBLANES, width), lambda bi, i: (bi, jnp.minimum((i + 1) * r, last), col_blk))
    return prev, cur, nxt


def _fill_window(win_ref, fn, prev_refs, cur_refs, next_refs, t):
    i = pl.program_id(1)
    n = pl.num_programs(1)
    h = BF16_SUBLANES
    load = lambda refs: [r[...].astype(F32) for r in refs]
    win_ref[0:h] = fn(*load(prev_refs)) * jnp.where(i > 0, 1.0, 0.0)
    win_ref[h:h + t] = fn(*load(cur_refs))
    win_ref[h + t:h + t + h] = fn(*load(next_refs)) * jnp.where(i < n - 1, 1.0, 0.0)


def _dw_conv(win_ref, wb_ref, dst_ref, t, n_taps):
    off = BF16_SUBLANES - n_taps // 2
    assert off >= 0 and off + n_taps - 1 + _CONV_ROWS <= _CONV_SLAB
    width = dst_ref.shape[-1]
    groups = _CONV_ROWS // 8

    def tile(ti, _):
        r0 = pl.multiple_of(ti * _CONV_ROWS, _CONV_ROWS)
        for c0 in range(0, width, _LANES):
            slab = win_ref[pl.ds(r0, _CONV_SLAB), c0:c0 + _LANES]
            acc = None
            for res in range(8):
                taps = [o for o in range(off, off + n_taps) if o % 8 == res]
                if not taps:
                    continue
                sh = slab if res == 0 else pltpu.roll(slab, _CONV_SLAB - res, axis=0)
                for o in taps:
                    q = 8 * (o // 8)
                    term = sh[q:q + _CONV_ROWS].reshape(groups, 8, _LANES) * wb_ref[o - off][:, c0:c0 + _LANES]
                    acc = term if acc is None else acc + term
            dst_ref[pl.ds(r0, _CONV_ROWS), c0:c0 + _LANES] = acc.reshape(_CONV_ROWS, _LANES)
        return 0

    lax.fori_loop(0, t // _CONV_ROWS, tile, 0)


def _conformer_kernel(vp, gp, vc, gc, vn, gn, s_ref, wb_ref, prm_ref, o_ref, win_ref, conv_ref, *, t, n_taps):
    _fill_window(win_ref, lambda v, g: v * _sigmoid(g), (vp, gp), (vc, gc), (vn, gn), t)
    _dw_conv(win_ref, wb_ref, conv_ref, t, n_taps)

    def tile(ti, _):
        r0 = pl.multiple_of(ti * _EPI_ROWS, _EPI_ROWS)
        acc = conv_ref[pl.ds(r0, _EPI_ROWS), :] + prm_ref[0:1, :]
        mu = jnp.mean(acc, axis=-1, keepdims=True)
        d = acc - mu
        var = jnp.mean(d * d, axis=-1, keepdims=True)
        yn = d * lax.rsqrt(var + EPS) * prm_ref[1:2, :] + prm_ref[2:3, :]
        s = s_ref[pl.ds(r0, _EPI_ROWS), :].astype(F32)
        o_ref[pl.ds(r0, _EPI_ROWS), :] = (_silu(yn) * _silu(s)).astype(o_ref.dtype)
        return 0

    trips = t // _EPI_ROWS
    lax.fori_loop(0, trips, tile, 0, unroll=2 if trips % 2 == 0 else 1)


def _shortconv_kernel(xp, cp, xc, cc, xn, cn, gb_ref, s_ref, wb_ref, o_ref, win_ref, conv_ref, *, t, n_taps):
    _fill_window(win_ref, lambda x, c: c * x, (xp, cp), (xc, cc), (xn, cn), t)
    _dw_conv(win_ref, wb_ref, conv_ref, t, n_taps)

    def tile(ti, _):
        r0 = pl.multiple_of(ti * _EPI_ROWS, _EPI_ROWS)
        gb = gb_ref[pl.ds(r0, _EPI_ROWS), :].astype(F32)
        s = s_ref[pl.ds(r0, _EPI_ROWS), :].astype(F32)
        o_ref[pl.ds(r0, _EPI_ROWS), :] = (gb * conv_ref[pl.ds(r0, _EPI_ROWS), :] * _silu(s)).astype(o_ref.dtype)
        return 0

    trips = t // _EPI_ROWS
    lax.fori_loop(0, trips, tile, 0, unroll=2 if trips % 2 == 0 else 1)


def _sublane_repeat(w):
    return jnp.broadcast_to(w.astype(F32)[:, None, :], (w.shape[0], 8, w.shape[1]))


def _conv_scratch(t, width):
    return [pltpu.VMEM((t + 2 * BF16_SUBLANES, width), F32), pltpu.VMEM((t, width), F32)]


_N_CONFORMER_IN = 9


def _local_mix_kernel(*refs, t, taps_cf, taps_sc):
    cf_in, sc_in = refs[:_N_CONFORMER_IN], refs[_N_CONFORMER_IN:-4]
    ob_ref, oc_ref, win_ref, conv_ref = refs[-4:]
    _conformer_kernel(*cf_in, ob_ref, win_ref, conv_ref, t=t, n_taps=taps_cf)
    _shortconv_kernel(*sc_in, oc_ref, win_ref, conv_ref, t=t, n_taps=taps_sc)


def _local_mixers(z, w_cf, prm_cf, w_sc, width, cf_cols, sc_cols):
    b, l, _ = z.shape
    t = _row_tile(l, 512)
    nb = l // t
    wb_cf = _sublane_repeat(w_cf)
    wb_sc = _sublane_repeat(w_sc)
    col_val, col_gl, col_sb = cf_cols
    col_x, col_gb, col_gc, col_sc = sc_cols
    vp, vc, vn = _halo_specs(t, width, col_val, nb)
    gp, gc, gn = _halo_specs(t, width, col_gl, nb)
    xp, xc, xn = _halo_specs(t, width, col_x, nb)
    cp, cc, cn = _halo_specs(t, width, col_gc, nb)
    rows = lambda col: pl.BlockSpec((None, t, width), lambda bi, i: (bi, i, col))
    whole = lambda a: pl.BlockSpec(a.shape, lambda bi, i: (0,) * a.ndim)
    cf_specs = [vp, gp, vc, gc, vn, gn, rows(col_sb), whole(wb_cf), whole(prm_cf)]
    sc_specs = [xp, cp, xc, cc, xn, cn, rows(col_gb), rows(col_sc), whole(wb_sc)]
    assert len(cf_specs) == _N_CONFORMER_IN
    out = jax.ShapeDtypeStruct((b, l, width), BF16)
    return pl.pallas_call(
        functools.partial(_local_mix_kernel, t=t, taps_cf=w_cf.shape[0], taps_sc=w_sc.shape[0]),
        grid=(b, nb),
        in_specs=cf_specs + sc_specs,
        out_specs=[rows(0), rows(0)],
        out_shape=[out, out],
        scratch_shapes=_conv_scratch(t, width),
        compiler_params=_params("parallel", "parallel"),
        name="local_mixers",
    )(*([z] * 7), wb_cf, prm_cf, *([z] * 8), wb_sc)


def _dft_split(n):
    n2 = 1 << (int(math.log2(n)) // 2)
    n1 = n // n2
    assert n1 * n2 == n
    return n1, n2


def _cos_sin(rows, cols, period):
    ang = 2.0 * np.pi * ((np.outer(np.arange(rows), np.arange(cols))) % period) / period
    return np.cos(ang), np.sin(ang)


def _fnet_tables(seq, group_width):
    n1, n2 = _dft_split(seq)
    cc, sc = _cos_sin(group_width, group_width, group_width)
    c1, s1 = (jnp.asarray(v, F32) for v in _cos_sin(n1, n1, n1))
    c2, s2 = _cos_sin(n2, n2, n2)
    tc, ts = (jnp.asarray(v, F32)[:, :, None] for v in _cos_sin(n2, n1, seq))
    cm = c1[None] * tc - s1[None] * ts
    sm = s1[None] * tc + c1[None] * ts
    m1 = jnp.concatenate([jnp.concatenate([cm, -sm], axis=2), jnp.concatenate([sm, cm], axis=2)], axis=1)
    return dict(
        cs=jnp.asarray(np.concatenate([cc, sc], axis=1), F32).astype(BF16),
        m1=m1.astype(BF16),
        m2=jnp.asarray(np.concatenate([c2, -s2], axis=1), F32).astype(BF16),
    )


def _fnet_stage1_kernel(x_ref, cs_ref, m1_ref, or_ref, oi_ref, *, groups):
    n1, nb, width = x_ref.shape
    gw = width // groups
    x = pltpu.einshape("abc->bac", x_ref[...]).reshape(nb * n1, width)
    ps, qs = [], []
    for g in range(groups):
        pq = jnp.dot(x[:, g * gw:(g + 1) * gw], cs_ref[...], preferred_element_type=F32)
        ps.append(pq[:, :gw])
        qs.append(pq[:, gw:])
    p = jnp.concatenate(ps, axis=1).astype(BF16)
    q = jnp.concatenate(qs, axis=1).astype(BF16)
    res_r, res_i = [], []
    for j in range(nb):
        pq = jnp.concatenate([p[j * n1:(j + 1) * n1], q[j * n1:(j + 1) * n1]], axis=0)
        a = jnp.dot(m1_ref[j], pq, preferred_element_type=F32)
        res_r.append(a[:n1].astype(or_ref.dtype))
        res_i.append(a[n1:].astype(oi_ref.dtype))
    or_ref[...] = pltpu.einshape("abc->bac", jnp.stack(res_r, axis=0))
    oi_ref[...] = pltpu.einshape("abc->bac", jnp.stack(res_i, axis=0))


def _fnet_stage2_kernel(ar_ref, ai_ref, m2_ref, s_ref, o_ref, *, scale):
    kb = ar_ref.shape[0]
    res = []
    for j in range(kb):
        rhs = jnp.concatenate([ar_ref[j], ai_ref[j]], axis=0)
        res.append(jnp.dot(m2_ref[...], rhs, preferred_element_type=F32) * scale)
    f = pltpu.einshape("abc->bac", jnp.stack(res, axis=0))
    o_ref[...] = (f * _silu(s_ref[...].astype(F32))).astype(o_ref.dtype)


def _fnet_mixer(zd, z, col_s, tables):
    b, l, width = zd.shape
    n1, n2 = _dft_split(l)
    gw = width // FNET_GROUPS
    nb = min(n2, BF16_SUBLANES)
    kb = min(n1, BF16_SUBLANES)
    x4 = zd.reshape(b, n1, n2, width)
    blk = pl.BlockSpec((None, n1, nb, width), lambda bi, j: (bi, 0, j, 0))
    m1_spec = pl.BlockSpec((nb, 2 * n1, 2 * n1), lambda bi, j: (j, 0, 0))
    full = lambda a: pl.BlockSpec(a.shape, lambda bi, j: (0,) * a.ndim)
    ar, ai = pl.pallas_call(
        functools.partial(_fnet_stage1_kernel, groups=FNET_GROUPS),
        grid=(b, n2 // nb),
        in_specs=[blk, full(tables["cs"]), m1_spec],
        out_specs=[blk, blk],
        out_shape=[jax.ShapeDtypeStruct((b, n1, n2, width), BF16)] * 2,
        compiler_params=_params("parallel", "parallel"),
        name="fnet_stage1",
    )(x4, tables["cs"], tables["m1"])
    blk2 = pl.BlockSpec((None, kb, n2, width), lambda bi, j: (bi, j, 0, 0))
    z4 = z.reshape(b, n2, n1, z.shape[-1])
    scale = 1.0 / math.sqrt(l * gw)
    f = pl.pallas_call(
        functools.partial(_fnet_stage2_kernel, scale=scale),
        grid=(b, n1 // kb),
        in_specs=[blk2, blk2, full(tables["m2"]),
                  pl.BlockSpec((None, n2, kb, width), lambda bi, j: (bi, 0, j, col_s))],
        out_specs=pl.BlockSpec((None, n2, kb, width), lambda bi, j: (bi, 0, j, 0)),
        out_shape=jax.ShapeDtypeStruct((b, n2, n1, width), BF16),
        compiler_params=_params("parallel", "parallel"),
        name="fnet_stage2",
    )(ar, ai, tables["m2"], z4)
    return f.reshape(b, l, width)


def _branch_kernel(pa, pb, pc, pd, w_ref, ga, gb, gc, gd, o_ref):
    acc = None
    for k, (p, g) in enumerate(((pa, ga), (pb, gb), (pc, gc), (pd, gd))):
        br = jnp.dot(p[...], w_ref[k], preferred_element_type=F32)
        term = _sigmoid(g[...].astype(F32)) * br
        acc = term if acc is None else acc + term
    o_ref[...] = acc.astype(o_ref.dtype)


def _branch_merge(ps, w_branch, layer, z2d, gate_col0, d_model):
    m, width = ps[0].shape
    tm = _row_tile(m, 1024)
    tn = _row_tile(d_model, 512)
    assert gate_col0 % tn == 0
    g0 = gate_col0 // tn
    per = d_model // tn
    p_spec = pl.BlockSpec((tm, width), lambda i, j: (i, 0))
    g_specs = [pl.BlockSpec((tm, tn), functools.partial(lambda i, j, k: (i, g0 + k * per + j), k=k))
               for k in range(N_BRANCH)]
    return pl.pallas_call(
        _branch_kernel,
        grid=(m // tm, d_model // tn),
        in_specs=[p_spec] * 4 + [pl.BlockSpec((None, N_BRANCH, width, tn), lambda i, j: (layer, 0, 0, j))] + g_specs,
        out_specs=pl.BlockSpec((tm, tn), lambda i, j: (i, j)),
        out_shape=jax.ShapeDtypeStruct((m, d_model), BF16),
        compiler_params=_params("parallel", "parallel"),
        name="branch_merge",
    )(*ps, w_branch, z2d, z2d, z2d, z2d)


def _out_norm_kernel(*refs, final):
    if final:
        m_ref, w_ref, x_ref, gate_ref, g_ref, h_ref, xrow_ref = refs
        o_ref = sc_ref = sh_ref = None
    else:
        m_ref, w_ref, x_ref, gate_ref, g_ref, sc_ref, sh_ref, o_ref, h_ref, xrow_ref = refs
    j = pl.program_id(2)
    nj = xrow_ref.shape[0] + 1
    tn = x_ref.shape[-1]

    def residual():
        xn = x_ref[...] + gate_ref[...] * jnp.dot(m_ref[...], w_ref[...], preferred_element_type=F32)
        if o_ref is not None:
            o_ref[...] = xn
        return xn

    @pl.when(j < nj - 1)
    def _():
        xrow_ref[j] = residual()

    @pl.when(j == nj - 1)
    def _():
        parts = [xrow_ref[jj] for jj in range(nj - 1)] + [residual()]
        ss = parts[0] * parts[0]
        for p in parts[1:]:
            ss = ss + p * p
        rstd = lax.rsqrt(jnp.sum(ss, axis=-1, keepdims=True) / (nj * tn) + EPS)
        for jj, p in enumerate(parts):
            sl = slice(jj * tn, (jj + 1) * tn)
            n = p * rstd * g_ref[:, sl]
            if not final:
                n = n * (1.0 + sc_ref[:, sl]) + sh_ref[:, sl]
            h_ref[:, sl] = n.astype(h_ref.dtype)


def _out_proj_residual_norm(merged, w_out, layer, x, gate, g_next, scale_next=None, shift_next=None):
    b, l, d = x.shape
    final = scale_next is None
    tm = _row_tile(l, 512)
    tn = _row_tile(d, 1024)
    nj = d // tn
    assert nj >= 2
    row = lambda bi, i, j: (bi, i, 0)
    vec = lambda bi, i, j: (bi, 0, 0)
    tile = lambda bi, i, j: (bi, i, j)
    in_specs = [pl.BlockSpec((None, tm, d), row),
                pl.BlockSpec((None, d, tn), lambda bi, i, j: (layer, 0, j)),
                pl.BlockSpec((None, tm, tn), tile),
                pl.BlockSpec((None, 1, tn), lambda bi, i, j: (bi, 0, j)),
                pl.BlockSpec((1, d), lambda bi, i, j: (0, 0))]
    args = [merged, w_out, x, gate, g_next]
    if final:
        out_specs = pl.BlockSpec((None, tm, d), row)
        out_shape = jax.ShapeDtypeStruct((b, l, d), x.dtype)
    else:
        in_specs += [pl.BlockSpec((None, 1, d), vec), pl.BlockSpec((None, 1, d), vec)]
        args += [scale_next, shift_next]
        out_specs = [pl.BlockSpec((None, tm, tn), tile), pl.BlockSpec((None, tm, d), row)]
        out_shape = [jax.ShapeDtypeStruct((b, l, d), x.dtype), jax.ShapeDtypeStruct((b, l, d), BF16)]
    return pl.pallas_call(
        functools.partial(_out_norm_kernel, final=final),
        grid=(b, l // tm, nj),
        in_specs=in_specs,
        out_specs=out_specs,
        out_shape=out_shape,
        scratch_shapes=[pltpu.VMEM((nj - 1, tm, tn), F32)],
        compiler_params=_params("parallel", "parallel", "arbitrary"),
        name="out_proj_final_norm" if final else "out_proj_residual_norm",
    )(*args)


def _block_diag_gates(wa, wx, cb):
    depth, _, heads, hd, _ = wa.shape
    hpb = cb // hd
    n_cb = heads // hpb
    eye = jnp.eye(hpb, dtype=wa.dtype)

    def bd(w):
        w = w.reshape(depth, 2, n_cb, hpb, hd, hd)
        full = jnp.einsum("ldchij,hg->ldchigj", w, eye)
        return full.reshape(depth, 2, n_cb, cb, cb)

    both = 0.5 * jnp.concatenate([bd(wa), bd(wx)], axis=-1)
    return both.transpose(0, 2, 1, 3, 4).astype(BF16)


def _token_mix(z, zd, y_a, width, d_model, lp, tables):
    b, l, n_main = z.shape
    y_b, y_c = _local_mixers(z, lp["cf_conv_w"], lp["cf_prm"], lp["sc_conv_w"], width, (1, 2, 7), (3, 4, 5, 8))
    y_d = _fnet_mixer(zd, z, 9, tables)
    m = b * l
    ps = [y.reshape(m, width) for y in (y_a, y_b, y_c, y_d)]
    merged = _branch_merge(ps, lp["w_branch"], lp["layer"], z.reshape(m, n_main), 10 * width, d_model)
    return merged.reshape(b, l, d_model)


def kernel(x, c, ctx, c_ctx, norm_g, w_mod, b_mod, w_in, lru_conv_w, lru_conv_b, lru_wa, lru_ba, lru_wx,
           lru_bx, lru_lambda, cf_conv_w, cf_conv_b, cf_ln_g, cf_ln_b, sc_conv_w, w_branch, w_out, final_g):
    bsz, seq, d_model = x.shape
    ctx_len = ctx.shape[1]
    depth = w_in.shape[0]
    width = d_model // N_BRANCH
    assert bsz + 1 <= 8

    four_blk = 6
    n_main_blk = w_in.shape[-1] // width - 1
    n_main = n_main_blk * width
    w_branch_b = w_branch.astype(BF16)
    w_out_b = w_out.astype(BF16)

    cb = min(256, width)
    wg = _block_diag_gates(lru_wa, lru_wx, cb)
    lru_prm = jnp.concatenate([lru_conv_w, lru_conv_b[:, :, None], 0.5 * lru_ba[:, :, None],
                               0.5 * lru_bx[:, :, None], lru_lambda[:, :, None]], axis=2).astype(F32)
    cf_prm = jnp.pad(jnp.stack([cf_conv_b, cf_ln_g, cf_ln_b], axis=1), ((0, 0), (0, 5), (0, 0)))
    tables_lat = _fnet_tables(seq, width // FNET_GROUPS)
    tables_ctx = _fnet_tables(ctx_len, width // FNET_GROUPS)

    cin = jnp.concatenate([c, c_ctx[None], jnp.zeros((8 - bsz - 1, d_model), c.dtype)], axis=0)
    mods = _adaln_mod(cin, w_mod, b_mod)

    xc = ctx
    s_blk = (6 * width) // cb
    rows_c = jnp.full((bsz,), bsz, jnp.int32)

    def mod_parts(l):
        shift, scale, gate = (mods[l, :, k * d_model:(k + 1) * d_model] for k in range(3))
        return dict(g=norm_g[l][None], gate=gate[:bsz, None], gate_c=gate[rows_c][:, None],
                    scale=scale[:bsz, None], shift=shift[:bsz, None],
                    scale_c=scale[rows_c][:, None], shift_c=shift[rows_c][:, None])

    mp = mod_parts(0)
    hc = _norm_mod(xc, mp["g"], mp["scale_c"], mp["shift_c"])
    h = _norm_mod(x, mp["g"], mp["scale"], mp["shift"])
    for l in range(depth):
        last = l == depth - 1
        lp = dict(cf_conv_w=cf_conv_w[l], cf_prm=cf_prm[l], sc_conv_w=sc_conv_w[l], w_branch=w_branch_b, layer=l)

        hc2 = hc.reshape(bsz * ctx_len, d_model)
        zc = _in_proj(hc2, w_in, l, 0, 1 if last else n_main_blk, four_blk, width, "in_proj_ctx")
        zc = zc.reshape(bsz, ctx_len, -1)
        h0 = jnp.zeros((bsz, 2, width), F32)
        yc_a, h_ctx = _lru_mixer(zc, s_blk, wg[l], lru_prm[l], h0, width, with_out=not last)

        h2 = h.reshape(bsz * seq, d_model)
        z = _in_proj(h2, w_in, l, 0, n_main_blk, four_blk, width, "in_proj").reshape(bsz, seq, n_main)
        zd = _in_proj(h2, w_in, l, four_blk, 1, None, width, "in_proj_fourier").reshape(bsz, seq, width)
        y_a, _ = _lru_mixer(z, s_blk, wg[l], lru_prm[l], h_ctx, width, with_out=True)
        merged = _token_mix(z, zd, y_a, width, d_model, lp, tables_lat)
        if last:
            return _out_proj_residual_norm(merged, w_out_b, l, x, mp["gate"], final_g[None])
        else:
            zcd = _in_proj(hc2, w_in, l, four_blk, 1, None, width, "in_proj_fourier_ctx")
            zcd = zcd.reshape(bsz, ctx_len, width)
            merged_c = _token_mix(zc, zcd, yc_a, width, d_model, lp, tables_ctx)
            nxt = mod_parts(l + 1)
            xc, hc = _out_proj_residual_norm(merged_c, w_out_b, l, xc, mp["gate_c"],
                                             nxt["g"], nxt["scale_c"], nxt["shift_c"])
            x, h = _out_proj_residual_norm(merged, w_out_b, l, x, mp["gate"],
                                           nxt["g"], nxt["scale"], nxt["shift"])
            mp = nxt
```

```python
import functools
import math

import numpy as np
import jax
import jax.numpy as jnp
from jax import lax
from jax.experimental import pallas as pl
from jax.experimental.pallas import tpu as pltpu

EPS = 1e-6
LRU_C = 8.0
N_BRANCH = 4
FNET_GROUPS = 4
V7X_VMEM_LIMIT_BYTES = 60 * 1024 * 1024
BF16_SUBLANES = 16
_LANES = 128
F32 = jnp.float32
BF16 = jnp.bfloat16


def _params(*sem):
    return pltpu.CompilerParams(dimension_semantics=sem, vmem_limit_bytes=V7X_VMEM_LIMIT_BYTES)


def _sigmoid(v):
    return 0.5 * jnp.tanh(0.5 * v) + 0.5


def _silu(v):
    return v * _sigmoid(v)


def _mod_kernel(c_ref, w_ref, b_ref, o_ref):
    s = _silu(c_ref[...]).astype(BF16)
    w = w_ref[0].astype(BF16)
    o_ref[0] = jnp.dot(s, w, preferred_element_type=F32) + b_ref[0]


def _adaln_mod(cin, w_mod, b_mod):
    depth, d, n = w_mod.shape
    tn = min(512, n)
    return pl.pallas_call(
        _mod_kernel,
        grid=(depth, n // tn),
        in_specs=[pl.BlockSpec((8, d), lambda l, j: (0, 0)),
                  pl.BlockSpec((1, d, tn), lambda l, j: (l, 0, j)),
                  pl.BlockSpec((1, 1, tn), lambda l, j: (l, 0, j))],
        out_specs=pl.BlockSpec((1, 8, tn), lambda l, j: (l, 0, j)),
        out_shape=jax.ShapeDtypeStruct((depth, 8, n), F32),
        compiler_params=_params("parallel", "parallel"),
        name="adaln_mod",
    )(cin, w_mod, b_mod.reshape(depth, 1, n))


def _norm_mod_kernel(x_ref, g_ref, sc_ref, sh_ref, o_ref):
    x = x_ref[...]
    n = x * lax.rsqrt(jnp.mean(x * x, axis=-1, keepdims=True) + EPS) * g_ref[...]
    o_ref[...] = (n * (1.0 + sc_ref[...]) + sh_ref[...]).astype(o_ref.dtype)


def _row_tile(l, cap):
    t = min(l, cap)
    while l % t:
        t //= 2
    return t


def _norm_mod(x, g, scale, shift):
    b, l, d = x.shape
    tr = _row_tile(l, 256)
    return pl.pallas_call(
        _norm_mod_kernel,
        grid=(b, l // tr),
        in_specs=[pl.BlockSpec((None, tr, d), lambda bi, i: (bi, i, 0)),
                  pl.BlockSpec((1, d), lambda bi, i: (0, 0)),
                  pl.BlockSpec((None, 1, d), lambda bi, i: (bi, 0, 0)),
                  pl.BlockSpec((None, 1, d), lambda bi, i: (bi, 0, 0))],
        out_specs=pl.BlockSpec((None, tr, d), lambda bi, i: (bi, i, 0)),
        out_shape=jax.ShapeDtypeStruct((b, l, d), BF16),
        compiler_params=_params("parallel", "parallel"),
        name="norm_mod",
    )(x, g, scale, shift)


def _in_proj_kernel(h_ref, w_hbm, o_ref, wf_ref, wb_ref, sem, *, layer, first_blk, skip_blk):
    j = pl.program_id(0)
    i = pl.program_id(1)
    tn = wb_ref.shape[-1]
    n_slots = wf_ref.shape[0]

    def panel_copy(jj):
        blk = first_blk + jj
        if skip_blk is not None:
            blk = blk + jnp.where(blk >= skip_blk, 1, 0)
        col = pl.multiple_of(blk * tn, tn)
        slot = jj % n_slots
        return pltpu.make_async_copy(w_hbm.at[layer, :, pl.ds(col, tn)], wf_ref.at[slot], sem.at[slot])

    def start_next():
        @pl.when(j + 1 < pl.num_programs(0))
        def _():
            panel_copy(j + 1).start()

    @pl.when(i == 0)
    def _():
        @pl.when(j == 0)
        def _():
            panel_copy(j).start()

        if n_slots > 1:
            start_next()
        panel_copy(j).wait()
        wb_ref[...] = wf_ref[j % n_slots].astype(BF16)
        if n_slots == 1:
            start_next()

    o_ref[...] = jnp.dot(h_ref[...], wb_ref[...], preferred_element_type=F32).astype(o_ref.dtype)


def _in_proj(h, w_in, layer, first_blk, n_blk, skip_blk, tn, name):
    m, k = h.shape
    tm = _row_tile(m, 1024)
    n_slots = 2 if m == tm else 1
    return pl.pallas_call(
        functools.partial(_in_proj_kernel, layer=layer, first_blk=first_blk, skip_blk=skip_blk),
        grid=(n_blk, m // tm),
        in_specs=[pl.BlockSpec((tm, k), lambda j, i: (i, 0)),
                  pl.BlockSpec(memory_space=pl.ANY)],
        out_specs=pl.BlockSpec((tm, tn), lambda j, i: (i, j)),
        out_shape=jax.ShapeDtypeStruct((m, n_blk * tn), BF16),
        scratch_shapes=[pltpu.VMEM((n_slots, k, tn), F32), pltpu.VMEM((k, tn), BF16),
                        pltpu.SemaphoreType.DMA((n_slots,))],
        compiler_params=_params("arbitrary", "arbitrary"),
        name=name,
    )(h, w_in)


_LRU_ROW_CONV_B, _LRU_ROW_BA, _LRU_ROW_BX, _LRU_ROW_LAMBDA = 4, 5, 6, 7


def _lru_kernel(*refs, seq, chunk, n_conv, with_out):
    if with_out:
        x_ref, s_ref, wg_ref, prm_ref, h0_ref, y_ref, hfin_ref, sa_ref, sb_ref, hf_ref, hb_ref = refs
    else:
        x_ref, wg_ref, prm_ref, h0_ref, hfin_ref, sa_ref, sb_ref = refs
    cb = x_ref.shape[-1]
    n_lb = cb // _LANES
    n_chunks = seq // chunk
    groups = chunk // 8
    n_gv = groups // 8
    halo = BF16_SUBLANES
    row = lax.broadcasted_iota(jnp.int32, (n_gv, 8, _LANES), 1)

    def gates(u, d):
        prm = prm_ref[d]
        g = jnp.dot(u.astype(BF16), wg_ref[d], preferred_element_type=F32)
        t_r = jnp.tanh(g[:, :cb] + prm[_LRU_ROW_BA:_LRU_ROW_BA + 1])
        t_i = jnp.tanh(g[:, cb:] + prm[_LRU_ROW_BX:_LRU_ROW_BX + 1])
        lam = prm[_LRU_ROW_LAMBDA:_LRU_ROW_LAMBDA + 1]
        c_half = (0.5 * LRU_C) * (jnp.minimum(lam, 0.0) - jnp.log1p(jnp.exp(-jnp.abs(lam))))
        log_a = c_half * t_r + c_half
        a = jnp.exp(log_a)
        th = jnp.tanh(log_a)
        q = (-0.5 * th) / (1.0 - th)
        bb = jnp.sqrt(q) * (u * (t_i + 1.0))
        return a, bb

    def conv(c, d):
        base = pl.multiple_of(c * chunk, chunk)
        cur = x_ref[pl.ds(base, chunk), :].astype(F32)
        prm = prm_ref[d]
        if d == 0:
            pbase = pl.multiple_of(jnp.maximum(base - halo, 0), halo)
            edge = x_ref[pl.ds(pbase, halo), :].astype(F32) * jnp.where(c > 0, 1.0, 0.0)
            win = jnp.concatenate([edge, cur], axis=0)
            off = halo - (n_conv - 1)
        else:
            nbase = pl.multiple_of(jnp.minimum(base + chunk, seq - halo), halo)
            edge = x_ref[pl.ds(nbase, halo), :].astype(F32) * jnp.where(c < n_chunks - 1, 1.0, 0.0)
            win = jnp.concatenate([cur, edge], axis=0)
            off = 0
        u = jnp.broadcast_to(prm[_LRU_ROW_CONV_B:_LRU_ROW_CONV_B + 1], (chunk, cb))
        for k in range(n_conv):
            u = u + win[off + k:off + k + chunk] * prm[k:k + 1]
        return u

    def scan(a, bb, carry, c, d, dst_ref):
        base = pl.multiple_of(c * chunk, chunk)
        order = range(8) if d == 0 else range(7, -1, -1)
        new_carry = []
        for lb in range(n_lb):
            sl = slice(lb * _LANES, (lb + 1) * _LANES)
            sa_ref[d, lb] = a[:, sl]
            sb_ref[d, lb] = bb[:, sl]
            hs, prods = {}, {}
            h = prod = None
            for s in order:
                a_s = sa_ref[d, lb, pl.ds(s, groups, stride=8), :]
                b_s = sb_ref[d, lb, pl.ds(s, groups, stride=8), :]
                h = b_s if h is None else a_s * h + b_s
                prod = a_s if prod is None else a_s * prod
                hs[s], prods[s] = h, prod
            a3 = prod.reshape(n_gv, 8, _LANES)
            b3 = h.reshape(n_gv, 8, _LANES)
            for st in (1, 2, 4):
                shift = st if d == 0 else 8 - st
                valid = (row >= st) if d == 0 else (row < 8 - st)
                a_sh = jnp.where(valid, pltpu.roll(a3, shift, axis=1), 1.0)
                b_sh = jnp.where(valid, pltpu.roll(b3, shift, axis=1), 0.0)
                b3 = a3 * b_sh + b3
                a3 = a3 * a_sh
            cur = carry[:, sl]
            cin = [None] * n_gv
            for v in (range(n_gv) if d == 0 else range(n_gv - 1, -1, -1)):
                after = a3[v] * cur + b3[v]
                if d == 0:
                    cin[v] = jnp.where(row[0] == 0, cur, pltpu.roll(after, 1, axis=0))
                    cur = after[7:8]
                else:
                    cin[v] = jnp.where(row[0] == 7, cur, pltpu.roll(after, 7, axis=0))
                    cur = after[0:1]
            new_carry.append(cur)
            if dst_ref is not None:
                cin = jnp.concatenate(cin, axis=0)
                for s in order:
                    dst_ref[lb, pl.ds(base + s, groups, stride=8), :] = hs[s] + prods[s] * cin
        return jnp.concatenate(new_carry, axis=1)

    def body(c, carry):
        cf, cbk = carry
        a, bb = gates(conv(c, 0), 0)
        cf = scan(a, bb, cf, c, 0, hf_ref if with_out else None)
        cr = n_chunks - 1 - c
        a, bb = gates(conv(cr, 1), 1)
        cbk = scan(a, bb, cbk, cr, 1, hb_ref if with_out else None)
        return cf, cbk

    h0 = h0_ref[...]
    cf, cbk = lax.fori_loop(0, n_chunks, body, (h0[0:1], h0[1:2]))
    hfin_ref[...] = jnp.concatenate([cf, cbk], axis=0)

    if with_out:
        def finish(c, _):
            base = pl.multiple_of(c * chunk, chunk)
            s = s_ref[pl.ds(base, chunk), :].astype(F32)
            y = jnp.concatenate([hf_ref[lb, pl.ds(base, chunk), :] + hb_ref[lb, pl.ds(base, chunk), :]
                                 for lb in range(n_lb)], axis=1)
            y_ref[pl.ds(base, chunk), :] = (y * _silu(s)).astype(y_ref.dtype)
            return 0
        lax.fori_loop(0, n_chunks, finish, 0)


def _lru_mixer(z, s_col_blk, wg, prm, h0, width, with_out):
    b, l, _ = z.shape
    n_cb, _, cb, _ = wg.shape
    n_conv = 4
    chunk = _row_tile(l, 256)
    kern = functools.partial(_lru_kernel, seq=l, chunk=chunk, n_conv=n_conv, with_out=with_out)
    x_spec = pl.BlockSpec((None, l, cb), lambda bi, ci: (bi, 0, ci))
    s_spec = pl.BlockSpec((None, l, cb), lambda bi, ci: (bi, 0, s_col_blk + ci))
    wg_spec = pl.BlockSpec((None, 2, cb, 2 * cb), lambda bi, ci: (ci, 0, 0, 0))
    prm_spec = pl.BlockSpec((2, 8, cb), lambda bi, ci: (0, 0, ci))
    st_spec = pl.BlockSpec((None, 2, cb), lambda bi, ci: (bi, 0, ci))
    st_shape = jax.ShapeDtypeStruct((b, 2, width), F32)
    n_lb = cb // _LANES
    assert cb % _LANES == 0 and chunk % 64 == 0
    stage = [pltpu.VMEM((2, n_lb, chunk, _LANES), F32), pltpu.VMEM((2, n_lb, chunk, _LANES), F32)]
    if with_out:
        y, hfin = pl.pallas_call(
            kern, grid=(b, n_cb),
            in_specs=[x_spec, s_spec, wg_spec, prm_spec, st_spec],
            out_specs=[pl.BlockSpec((None, l, cb), lambda bi, ci: (bi, 0, ci)), st_spec],
            out_shape=[jax.ShapeDtypeStruct((b, l, width), BF16), st_shape],
            scratch_shapes=stage + [pltpu.VMEM((n_lb, l, _LANES), F32), pltpu.VMEM((n_lb, l, _LANES), F32)],
            compiler_params=_params("parallel", "parallel"),
            name="rglru_mixer",
        )(z, z, wg, prm, h0)
        return y, hfin
    hfin = pl.pallas_call(
        kern, grid=(b, n_cb),
        in_specs=[x_spec, wg_spec, prm_spec, st_spec],
        out_specs=st_spec,
        out_shape=st_shape,
        scratch_shapes=stage,
        compiler_params=_params("parallel", "parallel"),
        name="rglru_state",
    )(z, wg, prm, h0)
    return None, hfin


_CONV_ROWS = 64
_CONV_SLAB = 96
_EPI_ROWS = 64


def _halo_specs(t, width, col_blk, n_row_blocks):
    r = t // BF16_SUBLANES
    last = n_row_blocks * r - 1
    cur = pl.BlockSpec((None, t, width), lambda bi, i: (bi, i, col_blk))
    prev = pl.BlockSpec((None, BF16_SUBLANES, width), lambda bi, i: (bi, jnp.maximum(i * r - 1, 0), col_blk))
    nxt = pl.BlockSpec((None, BF16_SUBLANES, width), lambda bi, i: (bi, jnp.minimum((i + 1) * r, last), col_blk))
    return prev, cur, nxt


def _fill_window(win_ref, fn, prev_refs, cur_refs, next_refs, t):
    i = pl.program_id(1)
    n = pl.num_programs(1)
    h = BF16_SUBLANES
    load = lambda refs: [r[...].astype(F32) for r in refs]
    win_ref[0:h] = fn(*load(prev_refs)) * jnp.where(i > 0, 1.0, 0.0)
    win_ref[h:h + t] = fn(*load(cur_refs))
    win_ref[h + t:h + t + h] = fn(*load(next_refs)) * jnp.where(i < n - 1, 1.0, 0.0)


def _dw_conv(win_ref, wb_ref, dst_ref, t, n_taps):
    off = BF16_SUBLANES - n_taps // 2
    assert off >= 0 and off + n_taps - 1 + _CONV_ROWS <= _CONV_SLAB
    width = dst_ref.shape[-1]
    groups = _CONV_ROWS // 8

    def tile(ti, _):
        r0 = pl.multiple_of(ti * _CONV_ROWS, _CONV_ROWS)
        for c0 in range(0, width, _LANES):
            slab = win_ref[pl.ds(r0, _CONV_SLAB), c0:c0 + _LANES]
            acc = None
            for res in range(8):
                taps = [o for o in range(off, off + n_taps) if o % 8 == res]
                if not taps:
                    continue
                sh = slab if res == 0 else pltpu.roll(slab, _CONV_SLAB - res, axis=0)
                for o in taps:
                    q = 8 * (o // 8)
                    term = sh[q:q + _CONV_ROWS].reshape(groups, 8, _LANES) * wb_ref[o - off][:, c0:c0 + _LANES]
                    acc = term if acc is None else acc + term
            dst_ref[pl.ds(r0, _CONV_ROWS), c0:c0 + _LANES] = acc.reshape(_CONV_ROWS, _LANES)
        return 0

    lax.fori_loop(0, t // _CONV_ROWS, tile, 0)


def _conformer_kernel(vp, gp, vc, gc, vn, gn, s_ref, wb_ref, prm_ref, o_ref, win_ref, conv_ref, *, t, n_taps):
    _fill_window(win_ref, lambda v, g: v * _sigmoid(g), (vp, gp), (vc, gc), (vn, gn), t)
    _dw_conv(win_ref, wb_ref, conv_ref, t, n_taps)

    def tile(ti, _):
        r0 = pl.multiple_of(ti * _EPI_ROWS, _EPI_ROWS)
        acc = conv_ref[pl.ds(r0, _EPI_ROWS), :] + prm_ref[0:1, :]
        mu = jnp.mean(acc, axis=-1, keepdims=True)
        d = acc - mu
        var = jnp.mean(d * d, axis=-1, keepdims=True)
        yn = d * lax.rsqrt(var + EPS) * prm_ref[1:2, :] + prm_ref[2:3, :]
        s = s_ref[pl.ds(r0, _EPI_ROWS), :].astype(F32)
        o_ref[pl.ds(r0, _EPI_ROWS), :] = (_silu(yn) * _silu(s)).astype(o_ref.dtype)
        return 0

    trips = t // _EPI_ROWS
    lax.fori_loop(0, trips, tile, 0, unroll=2 if trips % 2 == 0 else 1)


def _shortconv_kernel(xp, cp, xc, cc, xn, cn, gb_ref, s_ref, wb_ref, o_ref, win_ref, conv_ref, *, t, n_taps):
    _fill_window(win_ref, lambda x, c: c * x, (xp, cp), (xc, cc), (xn, cn), t)
    _dw_conv(win_ref, wb_ref, conv_ref, t, n_taps)

    def tile(ti, _):
        r0 = pl.multiple_of(ti * _EPI_ROWS, _EPI_ROWS)
        gb = gb_ref[pl.ds(r0, _EPI_ROWS), :].astype(F32)
        s = s_ref[pl.ds(r0, _EPI_ROWS), :].astype(F32)
        o_ref[pl.ds(r0, _EPI_ROWS), :] = (gb * conv_ref[pl.ds(r0, _EPI_ROWS), :] * _silu(s)).astype(o_ref.dtype)
        return 0

    trips = t // _EPI_ROWS
    lax.fori_loop(0, trips, tile, 0, unroll=2 if trips % 2 == 0 else 1)


def _sublane_repeat(w):
    return jnp.broadcast_to(w.astype(F32)[:, None, :], (w.shape[0], 8, w.shape[1]))


def _conv_scratch(t, width):
    return [pltpu.VMEM((t + 2 * BF16_SUBLANES, width), F32), pltpu.VMEM((t, width), F32)]


_N_CONFORMER_IN = 9


def _local_mix_kernel(*refs, t, taps_cf, taps_sc):
    cf_in, sc_in = refs[:_N_CONFORMER_IN], refs[_N_CONFORMER_IN:-4]
    ob_ref, oc_ref, win_ref, conv_ref = refs[-4:]
    _conformer_kernel(*cf_in, ob_ref, win_ref, conv_ref, t=t, n_taps=taps_cf)
    _shortconv_kernel(*sc_in, oc_ref, win_ref, conv_ref, t=t, n_taps=taps_sc)


def _local_mixers(z, w_cf, prm_cf, w_sc, width, cf_cols, sc_cols):
    b, l, _ = z.shape
    t = _row_tile(l, 512)
    nb = l // t
    wb_cf = _sublane_repeat(w_cf)
    wb_sc = _sublane_repeat(w_sc)
    col_val, col_gl, col_sb = cf_cols
    col_x, col_gb, col_gc, col_sc = sc_cols
    vp, vc, vn = _halo_specs(t, width, col_val, nb)
    gp, gc, gn = _halo_specs(t, width, col_gl, nb)
    xp, xc, xn = _halo_specs(t, width, col_x, nb)
    cp, cc, cn = _halo_specs(t, width, col_gc, nb)
    rows = lambda col: pl.BlockSpec((None, t, width), lambda bi, i: (bi, i, col))
    whole = lambda a: pl.BlockSpec(a.shape, lambda bi, i: (0,) * a.ndim)
    cf_specs = [vp, gp, vc, gc, vn, gn, rows(col_sb), whole(wb_cf), whole(prm_cf)]
    sc_specs = [xp, cp, xc, cc, xn, cn, rows(col_gb), rows(col_sc), whole(wb_sc)]
    assert len(cf_specs) == _N_CONFORMER_IN
    out = jax.ShapeDtypeStruct((b, l, width), BF16)
    return pl.pallas_call(
        functools.partial(_local_mix_kernel, t=t, taps_cf=w_cf.shape[0], taps_sc=w_sc.shape[0]),
        grid=(b, nb),
        in_specs=cf_specs + sc_specs,
        out_specs=[rows(0), rows(0)],
        out_shape=[out, out],
        scratch_shapes=_conv_scratch(t, width),
        compiler_params=_params("parallel", "parallel"),
        name="local_mixers",
    )(*([z] * 7), wb_cf, prm_cf, *([z] * 8), wb_sc)


def _dft_split(n):
    n2 = 1 << (int(math.log2(n)) // 2)
    n1 = n // n2
    assert n1 * n2 == n
    return n1, n2


def _cos_sin(rows, cols, period):
    ang = 2.0 * np.pi * ((np.outer(np.arange(rows), np.arange(cols))) % period) / period
    return np.cos(ang), np.sin(ang)


def _fnet_tables(seq, group_width):
    n1, n2 = _dft_split(seq)
    cc, sc = _cos_sin(group_width, group_width, group_width)
    c1, s1 = (jnp.asarray(v, F32) for v in _cos_sin(n1, n1, n1))
    c2, s2 = _cos_sin(n2, n2, n2)
    tc, ts = (jnp.asarray(v, F32)[:, :, None] for v in _cos_sin(n2, n1, seq))
    cm = c1[None] * tc - s1[None] * ts
    sm = s1[None] * tc + c1[None] * ts
    m1 = jnp.concatenate([jnp.concatenate([cm, -sm], axis=2), jnp.concatenate([sm, cm], axis=2)], axis=1)
    return dict(
        cs=jnp.asarray(np.concatenate([cc, sc], axis=1), F32).astype(BF16),
        m1=m1.astype(BF16),
        m2=jnp.asarray(np.concatenate([c2, -s2], axis=1), F32).astype(BF16),
    )


def _fnet_stage1_kernel(x_ref, cs_ref, m1_ref, or_ref, oi_ref, *, groups):
    n1, nb, width = x_ref.shape
    gw = width // groups
    x = pltpu.einshape("abc->bac", x_ref[...]).reshape(nb * n1, width)
    ps, qs = [], []
    for g in range(groups):
        pq = jnp.dot(x[:, g * gw:(g + 1) * gw], cs_ref[...], preferred_element_type=F32)
        ps.append(pq[:, :gw])
        qs.append(pq[:, gw:])
    p = jnp.concatenate(ps, axis=1).astype(BF16)
    q = jnp.concatenate(qs, axis=1).astype(BF16)
    res_r, res_i = [], []
    for j in range(nb):
        pq = jnp.concatenate([p[j * n1:(j + 1) * n1], q[j * n1:(j + 1) * n1]], axis=0)
        a = jnp.dot(m1_ref[j], pq, preferred_element_type=F32)
        res_r.append(a[:n1].astype(or_ref.dtype))
        res_i.append(a[n1:].astype(oi_ref.dtype))
    or_ref[...] = pltpu.einshape("abc->bac", jnp.stack(res_r, axis=0))
    oi_ref[...] = pltpu.einshape("abc->bac", jnp.stack(res_i, axis=0))


def _fnet_stage2_kernel(ar_ref, ai_ref, m2_ref, s_ref, o_ref, *, scale):
    kb = ar_ref.shape[0]
    res = []
    for j in range(kb):
        rhs = jnp.concatenate([ar_ref[j], ai_ref[j]], axis=0)
        res.append(jnp.dot(m2_ref[...], rhs, preferred_element_type=F32) * scale)
    f = pltpu.einshape("abc->bac", jnp.stack(res, axis=0))
    o_ref[...] = (f * _silu(s_ref[...].astype(F32))).astype(o_ref.dtype)


def _fnet_mixer(zd, z, col_s, tables):
    b, l, width = zd.shape
    n1, n2 = _dft_split(l)
    gw = width // FNET_GROUPS
    nb = min(n2, BF16_SUBLANES)
    kb = min(n1, BF16_SUBLANES)
    x4 = zd.reshape(b, n1, n2, width)
    blk = pl.BlockSpec((None, n1, nb, width), lambda bi, j: (bi, 0, j, 0))
    m1_spec = pl.BlockSpec((nb, 2 * n1, 2 * n1), lambda bi, j: (j, 0, 0))
    full = lambda a: pl.BlockSpec(a.shape, lambda bi, j: (0,) * a.ndim)
    ar, ai = pl.pallas_call(
        functools.partial(_fnet_stage1_kernel, groups=FNET_GROUPS),
        grid=(b, n2 // nb),
        in_specs=[blk, full(tables["cs"]), m1_spec],
        out_specs=[blk, blk],
        out_shape=[jax.ShapeDtypeStruct((b, n1, n2, width), BF16)] * 2,
        compiler_params=_params("parallel", "parallel"),
        name="fnet_stage1",
    )(x4, tables["cs"], tables["m1"])
    blk2 = pl.BlockSpec((None, kb, n2, width), lambda bi, j: (bi, j, 0, 0))
    z4 = z.reshape(b, n2, n1, z.shape[-1])
    scale = 1.0 / math.sqrt(l * gw)
    f = pl.pallas_call(
        functools.partial(_fnet_stage2_kernel, scale=scale),
        grid=(b, n1 // kb),
        in_specs=[blk2, blk2, full(tables["m2"]),
                  pl.BlockSpec((None, n2, kb, width), lambda bi, j: (bi, 0, j, col_s))],
        out_specs=pl.BlockSpec((None, n2, kb, width), lambda bi, j: (bi, 0, j, 0)),
        out_shape=jax.ShapeDtypeStruct((b, n2, n1, width), BF16),
        compiler_params=_params("parallel", "parallel"),
        name="fnet_stage2",
    )(ar, ai, tables["m2"], z4)
    return f.reshape(b, l, width)


def _branch_kernel(pa, pb, pc, pd, w_ref, ga, gb, gc, gd, o_ref):
    acc = None
    for k, (p, g) in enumerate(((pa, ga), (pb, gb), (pc, gc), (pd, gd))):
        br = jnp.dot(p[...], w_ref[k], preferred_element_type=F32)
        term = _sigmoid(g[...].astype(F32)) * br
        acc = term if acc is None else acc + term
    o_ref[...] = acc.astype(o_ref.dtype)


def _branch_merge(ps, w_branch, layer, z2d, gate_col0, d_model):
    m, width = ps[0].shape
    tm = _row_tile(m, 1024)
    tn = _row_tile(d_model, 1024)
    assert gate_col0 % tn == 0
    g0 = gate_col0 // tn
    per = d_model // tn
    p_spec = pl.BlockSpec((tm, width), lambda i, j: (i, 0))
    g_specs = [pl.BlockSpec((tm, tn), functools.partial(lambda i, j, k: (i, g0 + k * per + j), k=k))
               for k in range(N_BRANCH)]
    return pl.pallas_call(
        _branch_kernel,
        grid=(m // tm, d_model // tn),
        in_specs=[p_spec] * 4 + [pl.BlockSpec((None, N_BRANCH, width, tn), lambda i, j: (layer, 0, 0, j))] + g_specs,
        out_specs=pl.BlockSpec((tm, tn), lambda i, j: (i, j)),
        out_shape=jax.ShapeDtypeStruct((m, d_model), BF16),
        compiler_params=_params("parallel", "parallel"),
        name="branch_merge",
    )(*ps, w_branch, z2d, z2d, z2d, z2d)


def _out_norm_kernel(*refs, final):
    if final:
        m_ref, w_ref, x_ref, gate_ref, g_ref, h_ref, xrow_ref = refs
        o_ref = sc_ref = sh_ref = None
    else:
        m_ref, w_ref, x_ref, gate_ref, g_ref, sc_ref, sh_ref, o_ref, h_ref, xrow_ref = refs
    j = pl.program_id(2)
    nj = xrow_ref.shape[0] + 1
    tn = x_ref.shape[-1]

    def residual():
        xn = x_ref[...] + gate_ref[...] * jnp.dot(m_ref[...], w_ref[...], preferred_element_type=F32)
        if o_ref is not None:
            o_ref[...] = xn
        return xn

    @pl.when(j < nj - 1)
    def _():
        xrow_ref[j] = residual()

    @pl.when(j == nj - 1)
    def _():
        parts = [xrow_ref[jj] for jj in range(nj - 1)] + [residual()]
        ss = parts[0] * parts[0]
        for p in parts[1:]:
            ss = ss + p * p
        rstd = lax.rsqrt(jnp.sum(ss, axis=-1, keepdims=True) / (nj * tn) + EPS)
        for jj, p in enumerate(parts):
            sl = slice(jj * tn, (jj + 1) * tn)
            n = p * rstd * g_ref[:, sl]
            if not final:
                n = n * (1.0 + sc_ref[:, sl]) + sh_ref[:, sl]
            h_ref[:, sl] = n.astype(h_ref.dtype)


def _out_proj_residual_norm(merged, w_out, layer, x, gate, g_next, scale_next=None, shift_next=None):
    b, l, d = x.shape
    final = scale_next is None
    tm = _row_tile(l, 512)
    tn = _row_tile(d, 1024)
    nj = d // tn
    assert nj >= 2
    row = lambda bi, i, j: (bi, i, 0)
    vec = lambda bi, i, j: (bi, 0, 0)
    tile = lambda bi, i, j: (bi, i, j)
    in_specs = [pl.BlockSpec((None, tm, d), row),
                pl.BlockSpec((None, d, tn), lambda bi, i, j: (layer, 0, j)),
                pl.BlockSpec((None, tm, tn), tile),
                pl.BlockSpec((None, 1, tn), lambda bi, i, j: (bi, 0, j)),
                pl.BlockSpec((1, d), lambda bi, i, j: (0, 0))]
    args = [merged, w_out, x, gate, g_next]
    if final:
        out_specs = pl.BlockSpec((None, tm, d), row)
        out_shape = jax.ShapeDtypeStruct((b, l, d), x.dtype)
    else:
        in_specs += [pl.BlockSpec((None, 1, d), vec), pl.BlockSpec((None, 1, d), vec)]
        args += [scale_next, shift_next]
        out_specs = [pl.BlockSpec((None, tm, tn), tile), pl.BlockSpec((None, tm, d), row)]
        out_shape = [jax.ShapeDtypeStruct((b, l, d), x.dtype), jax.ShapeDtypeStruct((b, l, d), BF16)]
    return pl.pallas_call(
        functools.partial(_out_norm_kernel, final=final),
        grid=(b, l // tm, nj),
        in_specs=in_specs,
        out_specs=out_specs,
        out_shape=out_shape,
        scratch_shapes=[pltpu.VMEM((nj - 1, tm, tn), F32)],
        compiler_params=_params("parallel", "parallel", "arbitrary"),
        name="out_proj_final_norm" if final else "out_proj_residual_norm",
    )(*args)


def _block_diag_gates(wa, wx, cb):
    depth, _, heads, hd, _ = wa.shape
    hpb = cb // hd
    n_cb = heads // hpb
    eye = jnp.eye(hpb, dtype=wa.dtype)

    def bd(w):
        w = w.reshape(depth, 2, n_cb, hpb, hd, hd)
        full = jnp.einsum("ldchij,hg->ldchigj", w, eye)
        return full.reshape(depth, 2, n_cb, cb, cb)

    both = 0.5 * jnp.concatenate([bd(wa), bd(wx)], axis=-1)
    return both.transpose(0, 2, 1, 3, 4).astype(BF16)


def _token_mix(z, zd, y_a, width, d_model, lp, tables):
    b, l, n_main = z.shape
    y_b, y_c = _local_mixers(z, lp["cf_conv_w"], lp["cf_prm"], lp["sc_conv_w"], width, (1, 2, 7), (3, 4, 5, 8))
    y_d = _fnet_mixer(zd, z, 9, tables)
    m = b * l
    ps = [y.reshape(m, width) for y in (y_a, y_b, y_c, y_d)]
    merged = _branch_merge(ps, lp["w_branch"], lp["layer"], z.reshape(m, n_main), 10 * width, d_model)
    return merged.reshape(b, l, d_model)


def kernel(x, c, ctx, c_ctx, norm_g, w_mod, b_mod, w_in, lru_conv_w, lru_conv_b, lru_wa, lru_ba, lru_wx,
           lru_bx, lru_lambda, cf_conv_w, cf_conv_b, cf_ln_g, cf_ln_b, sc_conv_w, w_branch, w_out, final_g):
    bsz, seq, d_model = x.shape
    ctx_len = ctx.shape[1]
    depth = w_in.shape[0]
    width = d_model // N_BRANCH
    assert bsz + 1 <= 8

    four_blk = 6
    n_main_blk = w_in.shape[-1] // width - 1
    n_main = n_main_blk * width
    w_branch_b = w_branch.astype(BF16)
    w_out_b = w_out.astype(BF16)

    cb = min(256, width)
    wg = _block_diag_gates(lru_wa, lru_wx, cb)
    lru_prm = jnp.concatenate([lru_conv_w, lru_conv_b[:, :, None], 0.5 * lru_ba[:, :, None],
                               0.5 * lru_bx[:, :, None], lru_lambda[:, :, None]], axis=2).astype(F32)
    cf_prm = jnp.pad(jnp.stack([cf_conv_b, cf_ln_g, cf_ln_b], axis=1), ((0, 0), (0, 5), (0, 0)))
    tables_lat = _fnet_tables(seq, width // FNET_GROUPS)
    tables_ctx = _fnet_tables(ctx_len, width // FNET_GROUPS)

    cin = jnp.concatenate([c, c_ctx[None], jnp.zeros((8 - bsz - 1, d_model), c.dtype)], axis=0)
    mods = _adaln_mod(cin, w_mod, b_mod)

    xc = ctx
    s_blk = (6 * width) // cb
    rows_c = jnp.full((bsz,), bsz, jnp.int32)

    def mod_parts(l):
        shift, scale, gate = (mods[l, :, k * d_model:(k + 1) * d_model] for k in range(3))
        return dict(g=norm_g[l][None], gate=gate[:bsz, None], gate_c=gate[rows_c][:, None],
                    scale=scale[:bsz, None], shift=shift[:bsz, None],
                    scale_c=scale[rows_c][:, None], shift_c=shift[rows_c][:, None])

    mp = mod_parts(0)
    hc = _norm_mod(xc, mp["g"], mp["scale_c"], mp["shift_c"])
    h = _norm_mod(x, mp["g"], mp["scale"], mp["shift"])
    for l in range(depth):
        last = l == depth - 1
        lp = dict(cf_conv_w=cf_conv_w[l], cf_prm=cf_prm[l], sc_conv_w=sc_conv_w[l], w_branch=w_branch_b, layer=l)

        hc2 = hc.reshape(bsz * ctx_len, d_model)
        zc = _in_proj(hc2, w_in, l, 0, 1 if last else n_main_blk, four_blk, width, "in_proj_ctx")
        zc = zc.reshape(bsz, ctx_len, -1)
        h0 = jnp.zeros((bsz, 2, width), F32)
        yc_a, h_ctx = _lru_mixer(zc, s_blk, wg[l], lru_prm[l], h0, width, with_out=not last)

        h2 = h.reshape(bsz * seq, d_model)
        z = _in_proj(h2, w_in, l, 0, n_main_blk, four_blk, width, "in_proj").reshape(bsz, seq, n_main)
        zd = _in_proj(h2, w_in, l, four_blk, 1, None, width, "in_proj_fourier").reshape(bsz, seq, width)
        y_a, _ = _lru_mixer(z, s_blk, wg[l], lru_prm[l], h_ctx, width, with_out=True)
        merged = _token_mix(z, zd, y_a, width, d_model, lp, tables_lat)
        if last:
            return _out_proj_residual_norm(merged, w_out_b, l, x, mp["gate"], final_g[None])
        else:
            zcd = _in_proj(hc2, w_in, l, four_blk, 1, None, width, "in_proj_fourier_ctx")
            zcd = zcd.reshape(bsz, ctx_len, width)
            merged_c = _token_mix(zc, zcd, yc_a, width, d_model, lp, tables_ctx)
            nxt = mod_parts(l + 1)
            xc, hc = _out_proj_residual_norm(merged_c, w_out_b, l, xc, mp["gate_c"],
                                             nxt["g"], nxt["scale_c"], nxt["shift_c"])
            x, h = _out_proj_residual_norm(merged, w_out_b, l, x, mp["gate"],
                                           nxt["g"], nxt["scale"], nxt["shift"])
            mp = nxt
```

```python
import functools
import math

import numpy as np
import jax
import jax.numpy as jnp
from jax import lax
from jax.experimental import pallas as pl
from jax.experimental.pallas import tpu as pltpu

EPS = 1e-6
LRU_C = 8.0
N_BRANCH = 4
FNET_GROUPS = 4
V7X_VMEM_LIMIT_BYTES = 60 * 1024 * 1024
BF16_SUBLANES = 16
_LANES = 128
F32 = jnp.float32
BF16 = jnp.bfloat16


def _params(*sem):
    return pltpu.CompilerParams(dimension_semantics=sem, vmem_limit_bytes=V7X_VMEM_LIMIT_BYTES)


def _sigmoid(v):
    return 0.5 * jnp.tanh(0.5 * v) + 0.5


def _silu(v):
    return v * _sigmoid(v)


def _mod_kernel(c_ref, w_ref, b_ref, o_ref):
    s = _silu(c_ref[...]).astype(BF16)
    w = w_ref[0].astype(BF16)
    o_ref[0] = jnp.dot(s, w, preferred_element_type=F32) + b_ref[0]


def _adaln_mod(cin, w_mod, b_mod):
    depth, d, n = w_mod.shape
    tn = min(512, n)
    return pl.pallas_call(
        _mod_kernel,
        grid=(depth, n // tn),
        in_specs=[pl.BlockSpec((8, d), lambda l, j: (0, 0)),
                  pl.BlockSpec((1, d, tn), lambda l, j: (l, 0, j)),
                  pl.BlockSpec((1, 1, tn), lambda l, j: (l, 0, j))],
        out_specs=pl.BlockSpec((1, 8, tn), lambda l, j: (l, 0, j)),
        out_shape=jax.ShapeDtypeStruct((depth, 8, n), F32),
        compiler_params=_params("parallel", "parallel"),
        name="adaln_mod",
    )(cin, w_mod, b_mod.reshape(depth, 1, n))


def _norm_mod_kernel(x_ref, g_ref, sc_ref, sh_ref, o_ref):
    x = x_ref[...]
    n = x * lax.rsqrt(jnp.mean(x * x, axis=-1, keepdims=True) + EPS) * g_ref[...]
    o_ref[...] = (n * (1.0 + sc_ref[...]) + sh_ref[...]).astype(o_ref.dtype)


def _row_tile(l, cap):
    t = min(l, cap)
    while l % t:
        t //= 2
    return t


def _norm_mod(x, g, scale, shift):
    b, l, d = x.shape
    tr = _row_tile(l, 256)
    return pl.pallas_call(
        _norm_mod_kernel,
        grid=(b, l // tr),
        in_specs=[pl.BlockSpec((None, tr, d), lambda bi, i: (bi, i, 0)),
                  pl.BlockSpec((1, d), lambda bi, i: (0, 0)),
                  pl.BlockSpec((None, 1, d), lambda bi, i: (bi, 0, 0)),
                  pl.BlockSpec((None, 1, d), lambda bi, i: (bi, 0, 0))],
        out_specs=pl.BlockSpec((None, tr, d), lambda bi, i: (bi, i, 0)),
        out_shape=jax.ShapeDtypeStruct((b, l, d), BF16),
        compiler_params=_params("parallel", "parallel"),
        name="norm_mod",
    )(x, g, scale, shift)


def _in_proj_kernel(h_ref, w_hbm, o_ref, wf_ref, wb_ref, sem, *, layer, first_blk, skip_blk):
    j = pl.program_id(0)
    i = pl.program_id(1)
    tn = wb_ref.shape[-1]
    n_slots = wf_ref.shape[0]

    def panel_copy(jj):
        blk = first_blk + jj
        if skip_blk is not None:
            blk = blk + jnp.where(blk >= skip_blk, 1, 0)
        col = pl.multiple_of(blk * tn, tn)
        slot = jj % n_slots
        return pltpu.make_async_copy(w_hbm.at[layer, :, pl.ds(col, tn)], wf_ref.at[slot], sem.at[slot])

    def start_next():
        @pl.when(j + 1 < pl.num_programs(0))
        def _():
            panel_copy(j + 1).start()

    @pl.when(i == 0)
    def _():
        @pl.when(j == 0)
        def _():
            panel_copy(j).start()

        if n_slots > 1:
            start_next()
        panel_copy(j).wait()
        wb_ref[...] = wf_ref[j % n_slots].astype(BF16)
        if n_slots == 1:
            start_next()

    o_ref[...] = jnp.dot(h_ref[...], wb_ref[...], preferred_element_type=F32).astype(o_ref.dtype)


def _in_proj(h, w_in, layer, first_blk, n_blk, skip_blk, tn, name):
    m, k = h.shape
    tm = _row_tile(m, 1024)
    n_slots = 2 if m == tm else 1
    return pl.pallas_call(
        functools.partial(_in_proj_kernel, layer=layer, first_blk=first_blk, skip_blk=skip_blk),
        grid=(n_blk, m // tm),
        in_specs=[pl.BlockSpec((tm, k), lambda j, i: (i, 0)),
                  pl.BlockSpec(memory_space=pl.ANY)],
        out_specs=pl.BlockSpec((tm, tn), lambda j, i: (i, j)),
        out_shape=jax.ShapeDtypeStruct((m, n_blk * tn), BF16),
        scratch_shapes=[pltpu.VMEM((n_slots, k, tn), F32), pltpu.VMEM((k, tn), BF16),
                        pltpu.SemaphoreType.DMA((n_slots,))],
        compiler_params=_params("arbitrary", "arbitrary"),
        name=name,
    )(h, w_in)


_LRU_ROW_CONV_B, _LRU_ROW_BA, _LRU_ROW_BX, _LRU_ROW_LAMBDA = 4, 5, 6, 7


def _lru_kernel(*refs, seq, chunk, n_conv, with_out):
    if with_out:
        x_ref, s_ref, wg_ref, prm_ref, h0_ref, y_ref, hfin_ref, xs_ref, hf_ref, hb_ref = refs
    else:
        x_ref, wg_ref, prm_ref, h0_ref, hfin_ref, xs_ref = refs
    cb = x_ref.shape[-1]
    n_lb = cb // _LANES
    n_chunks = seq // chunk
    groups = chunk // 8
    n_gv = groups // 8
    halo = BF16_SUBLANES
    row = lax.broadcasted_iota(jnp.int32, (n_gv, 8, _LANES), 1)

    def gates(u, d):
        prm = prm_ref[d]
        g = jnp.dot(u.astype(BF16), wg_ref[d], preferred_element_type=F32)
        t_r = jnp.tanh(g[:, :cb] + prm[_LRU_ROW_BA:_LRU_ROW_BA + 1])
        t_i = jnp.tanh(g[:, cb:] + prm[_LRU_ROW_BX:_LRU_ROW_BX + 1])
        lam = prm[_LRU_ROW_LAMBDA:_LRU_ROW_LAMBDA + 1]
        c_half = (0.5 * LRU_C) * (jnp.minimum(lam, 0.0) - jnp.log1p(jnp.exp(-jnp.abs(lam))))
        log_a = c_half * t_r + c_half
        a = jnp.exp(log_a)
        th = jnp.tanh(log_a)
        q = (-0.5 * th) / (1.0 - th)
        bb = jnp.sqrt(q) * (u * (t_i + 1.0))
        return a, bb

    def conv(c, d):
        base = pl.multiple_of(c * chunk, chunk)
        cur = x_ref[pl.ds(base, chunk), :].astype(F32)
        if d == 0:
            pbase = pl.multiple_of(jnp.maximum(base - halo, 0), halo)
            edge = x_ref[pl.ds(pbase, halo), :].astype(F32)[halo - 8:] * jnp.where(c > 0, 1.0, 0.0)
            first = -(n_conv - 1)
        else:
            nbase = pl.multiple_of(jnp.minimum(base + chunk, seq - halo), halo)
            edge = x_ref[pl.ds(nbase, halo), :].astype(F32)[:8] * jnp.where(c < n_chunks - 1, 1.0, 0.0)
            first = 0
        prm = prm_ref[d]
        cols = []
        for lb in range(n_lb):
            sl = slice(lb * _LANES, (lb + 1) * _LANES)
            xs_ref[d, lb, 8:8 + chunk] = cur[:, sl]
            if d == 0:
                xs_ref[d, lb, 0:8] = edge[:, sl]
            else:
                xs_ref[d, lb, 8 + chunk:16 + chunk] = edge[:, sl]
            xm = [xs_ref[d, lb, pl.ds(8 + first + m, groups, stride=8), :] for m in range(8 + n_conv - 1)]
            parts = []
            for s in range(8):
                u = prm[_LRU_ROW_CONV_B:_LRU_ROW_CONV_B + 1, sl] + xm[s] * prm[0:1, sl]
                for k in range(1, n_conv):
                    u = u + xm[s + k] * prm[k:k + 1, sl]
                parts.append(u)
            cols.append(jnp.concatenate(parts, axis=0))
        return jnp.concatenate(cols, axis=1)

    def scan(a, bb, carry, c, d, dst_ref):
        base = pl.multiple_of(c * chunk, chunk)
        order = range(8) if d == 0 else range(7, -1, -1)
        new_carry = []
        for lb in range(n_lb):
            sl = slice(lb * _LANES, (lb + 1) * _LANES)
            hs, prods = {}, {}
            h = prod = None
            for s in order:
                a_s = a[s * groups:(s + 1) * groups, sl]
                b_s = bb[s * groups:(s + 1) * groups, sl]
                h = b_s if h is None else a_s * h + b_s
                prod = a_s if prod is None else a_s * prod
                hs[s], prods[s] = h, prod
            a3 = prod.reshape(n_gv, 8, _LANES)
            b3 = h.reshape(n_gv, 8, _LANES)
            for st in (1, 2, 4):
                shift = st if d == 0 else 8 - st
                valid = (row >= st) if d == 0 else (row < 8 - st)
                a_sh = jnp.where(valid, pltpu.roll(a3, shift, axis=1), 1.0)
                b_sh = jnp.where(valid, pltpu.roll(b3, shift, axis=1), 0.0)
                b3 = a3 * b_sh + b3
                a3 = a3 * a_sh
            cur = carry[:, sl]
            cin = [None] * n_gv
            for v in (range(n_gv) if d == 0 else range(n_gv - 1, -1, -1)):
                after = a3[v] * cur + b3[v]
                if d == 0:
                    cin[v] = jnp.where(row[0] == 0, cur, pltpu.roll(after, 1, axis=0))
                    cur = after[7:8]
                else:
                    cin[v] = jnp.where(row[0] == 7, cur, pltpu.roll(after, 7, axis=0))
                    cur = after[0:1]
            new_carry.append(cur)
            if dst_ref is not None:
                cin = jnp.concatenate(cin, axis=0)
                for s in order:
                    dst_ref[lb, pl.ds(base + s, groups, stride=8), :] = hs[s] + prods[s] * cin
        return jnp.concatenate(new_carry, axis=1)

    def body(c, carry):
        cf, cbk = carry
        a, bb = gates(conv(c, 0), 0)
        cf = scan(a, bb, cf, c, 0, hf_ref if with_out else None)
        cr = n_chunks - 1 - c
        a, bb = gates(conv(cr, 1), 1)
        cbk = scan(a, bb, cbk, cr, 1, hb_ref if with_out else None)
        return cf, cbk

    h0 = h0_ref[...]
    cf, cbk = lax.fori_loop(0, n_chunks, body, (h0[0:1], h0[1:2]))
    hfin_ref[...] = jnp.concatenate([cf, cbk], axis=0)

    if with_out:
        def finish(c, _):
            base = pl.multiple_of(c * chunk, chunk)
            s = s_ref[pl.ds(base, chunk), :].astype(F32)
            y = jnp.concatenate([hf_ref[lb, pl.ds(base, chunk), :] + hb_ref[lb, pl.ds(base, chunk), :]
                                 for lb in range(n_lb)], axis=1)
            y_ref[pl.ds(base, chunk), :] = (y * _silu(s)).astype(y_ref.dtype)
            return 0
        lax.fori_loop(0, n_chunks, finish, 0)


def _lru_mixer(z, s_col_blk, wg, prm, h0, width, with_out):
    b, l, _ = z.shape
    n_cb, _, cb, _ = wg.shape
    n_conv = 4
    chunk = _row_tile(l, 256)
    kern = functools.partial(_lru_kernel, seq=l, chunk=chunk, n_conv=n_conv, with_out=with_out)
    x_spec = pl.BlockSpec((None, l, cb), lambda bi, ci: (bi, 0, ci))
    s_spec = pl.BlockSpec((None, l, cb), lambda bi, ci: (bi, 0, s_col_blk + ci))
    wg_spec = pl.BlockSpec((None, 2, cb, 2 * cb), lambda bi, ci: (ci, 0, 0, 0))
    prm_spec = pl.BlockSpec((2, 8, cb), lambda bi, ci: (0, 0, ci))
    st_spec = pl.BlockSpec((None, 2, cb), lambda bi, ci: (bi, 0, ci))
    st_shape = jax.ShapeDtypeStruct((b, 2, width), F32)
    n_lb = cb // _LANES
    assert cb % _LANES == 0 and chunk % 64 == 0
    stage = [pltpu.VMEM((2, n_lb, chunk + 16, _LANES), F32)]
    if with_out:
        y, hfin = pl.pallas_call(
            kern, grid=(b, n_cb),
            in_specs=[x_spec, s_spec, wg_spec, prm_spec, st_spec],
            out_specs=[pl.BlockSpec((None, l, cb), lambda bi, ci: (bi, 0, ci)), st_spec],
            out_shape=[jax.ShapeDtypeStruct((b, l, width), BF16), st_shape],
            scratch_shapes=stage + [pltpu.VMEM((n_lb, l, _LANES), F32), pltpu.VMEM((n_lb, l, _LANES), F32)],
            compiler_params=_params("parallel", "parallel"),
            name="rglru_mixer",
        )(z, z, wg, prm, h0)
        return y, hfin
    hfin = pl.pallas_call(
        kern, grid=(b, n_cb),
        in_specs=[x_spec, wg_spec, prm_spec, st_spec],
        out_specs=st_spec,
        out_shape=st_shape,
        scratch_shapes=stage,
        compiler_params=_params("parallel", "parallel"),
        name="rglru_state",
    )(z, wg, prm, h0)
    return None, hfin


_CONV_ROWS = 64
_CONV_SLAB = 96
_EPI_ROWS = 64


def _halo_specs(t, width, col_blk, n_row_blocks):
    r = t // BF16_SUBLANES
    last = n_row_blocks * r - 1
    cur = pl.BlockSpec((None, t, width), lambda bi, i: (bi, i, col_blk))
    prev = pl.BlockSpec((None, BF16_SUBLANES, width), lambda bi, i: (bi, jnp.maximum(i * r - 1, 0), col_blk))
    nxt = pl.BlockSpec((None, BF16_SUBLANES, width), lambda bi, i: (bi, jnp.minimum((i + 1) * r, last), col_blk))
    return prev, cur, nxt


def _fill_window(win_ref, fn, prev_refs, cur_refs, next_refs, t):
    i = pl.program_id(1)
    n = pl.num_programs(1)
    h = BF16_SUBLANES
    load = lambda refs: [r[...].astype(F32) for r in refs]
    win_ref[0:h] = fn(*load(prev_refs)) * jnp.where(i > 0, 1.0, 0.0)
    win_ref[h:h + t] = fn(*load(cur_refs))
    win_ref[h + t:h + t + h] = fn(*load(next_refs)) * jnp.where(i < n - 1, 1.0, 0.0)


def _dw_conv(win_ref, wb_ref, dst_ref, t, n_taps):
    off = BF16_SUBLANES - n_taps // 2
    assert off >= 0 and off + n_taps - 1 + _CONV_ROWS <= _CONV_SLAB
    width = dst_ref.shape[-1]
    groups = _CONV_ROWS // 8

    def tile(ti, _):
        r0 = pl.multiple_of(ti * _CONV_ROWS, _CONV_ROWS)
        for c0 in range(0, width, _LANES):
            slab = win_ref[pl.ds(r0, _CONV_SLAB), c0:c0 + _LANES]
            acc = None
            for res in range(8):
                taps = [o for o in range(off, off + n_taps) if o % 8 == res]
                if not taps:
                    continue
                sh = slab if res == 0 else pltpu.roll(slab, _CONV_SLAB - res, axis=0)
                for o in taps:
                    q = 8 * (o // 8)
                    term = sh[q:q + _CONV_ROWS].reshape(groups, 8, _LANES) * wb_ref[o - off][:, c0:c0 + _LANES]
                    acc = term if acc is None else acc + term
            dst_ref[pl.ds(r0, _CONV_ROWS), c0:c0 + _LANES] = acc.reshape(_CONV_ROWS, _LANES)
        return 0

    lax.fori_loop(0, t // _CONV_ROWS, tile, 0)


def _conformer_kernel(vp, gp, vc, gc, vn, gn, s_ref, wb_ref, prm_ref, o_ref, win_ref, conv_ref, *, t, n_taps):
    _fill_window(win_ref, lambda v, g: v * _sigmoid(g), (vp, gp), (vc, gc), (vn, gn), t)
    _dw_conv(win_ref, wb_ref, conv_ref, t, n_taps)

    def tile(ti, _):
        r0 = pl.multiple_of(ti * _EPI_ROWS, _EPI_ROWS)
        acc = conv_ref[pl.ds(r0, _EPI_ROWS), :] + prm_ref[0:1, :]
        mu = jnp.mean(acc, axis=-1, keepdims=True)
        d = acc - mu
        var = jnp.mean(d * d, axis=-1, keepdims=True)
        yn = d * lax.rsqrt(var + EPS) * prm_ref[1:2, :] + prm_ref[2:3, :]
        s = s_ref[pl.ds(r0, _EPI_ROWS), :].astype(F32)
        o_ref[pl.ds(r0, _EPI_ROWS), :] = (_silu(yn) * _silu(s)).astype(o_ref.dtype)
        return 0

    trips = t // _EPI_ROWS
    lax.fori_loop(0, trips, tile, 0, unroll=2 if trips % 2 == 0 else 1)


def _shortconv_kernel(xp, cp, xc, cc, xn, cn, gb_ref, s_ref, wb_ref, o_ref, win_ref, conv_ref, *, t, n_taps):
    _fill_window(win_ref, lambda x, c: c * x, (xp, cp), (xc, cc), (xn, cn), t)
    _dw_conv(win_ref, wb_ref, conv_ref, t, n_taps)

    def tile(ti, _):
        r0 = pl.multiple_of(ti * _EPI_ROWS, _EPI_ROWS)
        gb = gb_ref[pl.ds(r0, _EPI_ROWS), :].astype(F32)
        s = s_ref[pl.ds(r0, _EPI_ROWS), :].astype(F32)
        o_ref[pl.ds(r0, _EPI_ROWS), :] = (gb * conv_ref[pl.ds(r0, _EPI_ROWS), :] * _silu(s)).astype(o_ref.dtype)
        return 0

    trips = t // _EPI_ROWS
    lax.fori_loop(0, trips, tile, 0, unroll=2 if trips % 2 == 0 else 1)


def _sublane_repeat(w):
    return jnp.broadcast_to(w.astype(F32)[:, None, :], (w.shape[0], 8, w.shape[1]))


def _conv_scratch(t, width):
    return [pltpu.VMEM((t + 2 * BF16_SUBLANES, width), F32), pltpu.VMEM((t, width), F32)]


_N_CONFORMER_IN = 9


def _local_mix_kernel(*refs, t, taps_cf, taps_sc):
    cf_in, sc_in = refs[:_N_CONFORMER_IN], refs[_N_CONFORMER_IN:-4]
    ob_ref, oc_ref, win_ref, conv_ref = refs[-4:]
    _conformer_kernel(*cf_in, ob_ref, win_ref, conv_ref, t=t, n_taps=taps_cf)
    _shortconv_kernel(*sc_in, oc_ref, win_ref, conv_ref, t=t, n_taps=taps_sc)


def _local_mixers(z, w_cf, prm_cf, w_sc, width, cf_cols, sc_cols):
    b, l, _ = z.shape
    t = _row_tile(l, 512)
    nb = l // t
    wb_cf = _sublane_repeat(w_cf)
    wb_sc = _sublane_repeat(w_sc)
    col_val, col_gl, col_sb = cf_cols
    col_x, col_gb, col_gc, col_sc = sc_cols
    vp, vc, vn = _halo_specs(t, width, col_val, nb)
    gp, gc, gn = _halo_specs(t, width, col_gl, nb)
    xp, xc, xn = _halo_specs(t, width, col_x, nb)
    cp, cc, cn = _halo_specs(t, width, col_gc, nb)
    rows = lambda col: pl.BlockSpec((None, t, width), lambda bi, i: (bi, i, col))
    whole = lambda a: pl.BlockSpec(a.shape, lambda bi, i: (0,) * a.ndim)
    cf_specs = [vp, gp, vc, gc, vn, gn, rows(col_sb), whole(wb_cf), whole(prm_cf)]
    sc_specs = [xp, cp, xc, cc, xn, cn, rows(col_gb), rows(col_sc), whole(wb_sc)]
    assert len(cf_specs) == _N_CONFORMER_IN
    out = jax.ShapeDtypeStruct((b, l, width), BF16)
    return pl.pallas_call(
        functools.partial(_local_mix_kernel, t=t, taps_cf=w_cf.shape[0], taps_sc=w_sc.shape[0]),
        grid=(b, nb),
        in_specs=cf_specs + sc_specs,
        out_specs=[rows(0), rows(0)],
        out_shape=[out, out],
        scratch_shapes=_conv_scratch(t, width),
        compiler_params=_params("parallel", "parallel"),
        name="local_mixers",
    )(*([z] * 7), wb_cf, prm_cf, *([z] * 8), wb_sc)


def _dft_split(n):
    n2 = 1 << (int(math.log2(n)) // 2)
    n1 = n // n2
    assert n1 * n2 == n
    return n1, n2


def _cos_sin(rows, cols, period):
    ang = 2.0 * np.pi * ((np.outer(np.arange(rows), np.arange(cols))) % period) / period
    return np.cos(ang), np.sin(ang)


def _fnet_tables(seq, group_width):
    n1, n2 = _dft_split(seq)
    cc, sc = _cos_sin(group_width, group_width, group_width)
    c1, s1 = (jnp.asarray(v, F32) for v in _cos_sin(n1, n1, n1))
    c2, s2 = _cos_sin(n2, n2, n2)
    tc, ts = (jnp.asarray(v, F32)[:, :, None] for v in _cos_sin(n2, n1, seq))
    cm = c1[None] * tc - s1[None] * ts
    sm = s1[None] * tc + c1[None] * ts
    m1 = jnp.concatenate([jnp.concatenate([cm, -sm], axis=2), jnp.concatenate([sm, cm], axis=2)], axis=1)
    return dict(
        cs=jnp.asarray(np.concatenate([cc, sc], axis=1), F32).astype(BF16),
        m1=m1.astype(BF16),
        m2=jnp.asarray(np.concatenate([c2, -s2], axis=1), F32).astype(BF16),
    )


def _fnet_stage1_kernel(x_ref, cs_ref, m1_ref, or_ref, oi_ref, *, groups):
    n1, nb, width = x_ref.shape
    gw = width // groups
    x = pltpu.einshape("abc->bac", x_ref[...]).reshape(nb * n1, width)
    ps, qs = [], []
    for g in range(groups):
        pq = jnp.dot(x[:, g * gw:(g + 1) * gw], cs_ref[...], preferred_element_type=F32)
        ps.append(pq[:, :gw])
        qs.append(pq[:, gw:])
    p = jnp.concatenate(ps, axis=1).astype(BF16)
    q = jnp.concatenate(qs, axis=1).astype(BF16)
    res_r, res_i = [], []
    for j in range(nb):
        pq = jnp.concatenate([p[j * n1:(j + 1) * n1], q[j * n1:(j + 1) * n1]], axis=0)
        a = jnp.dot(m1_ref[j], pq, preferred_element_type=F32)
        res_r.append(a[:n1].astype(or_ref.dtype))
        res_i.append(a[n1:].astype(oi_ref.dtype))
    or_ref[...] = pltpu.einshape("abc->bac", jnp.stack(res_r, axis=0))
    oi_ref[...] = pltpu.einshape("abc->bac", jnp.stack(res_i, axis=0))


def _fnet_stage2_kernel(ar_ref, ai_ref, m2_ref, s_ref, o_ref, *, scale):
    kb = ar_ref.shape[0]
    res = []
    for j in range(kb):
        rhs = jnp.concatenate([ar_ref[j], ai_ref[j]], axis=0)
        res.append(jnp.dot(m2_ref[...], rhs, preferred_element_type=F32) * scale)
    f = pltpu.einshape("abc->bac", jnp.stack(res, axis=0))
    o_ref[...] = (f * _silu(s_ref[...].astype(F32))).astype(o_ref.dtype)


def _fnet_mixer(zd, z, col_s, tables):
    b, l, width = zd.shape
    n1, n2 = _dft_split(l)
    gw = width // FNET_GROUPS
    nb = min(n2, BF16_SUBLANES)
    kb = min(n1, BF16_SUBLANES)
    x4 = zd.reshape(b, n1, n2, width)
    blk = pl.BlockSpec((None, n1, nb, width), lambda bi, j: (bi, 0, j, 0))
    m1_spec = pl.BlockSpec((nb, 2 * n1, 2 * n1), lambda bi, j: (j, 0, 0))
    full = lambda a: pl.BlockSpec(a.shape, lambda bi, j: (0,) * a.ndim)
    ar, ai = pl.pallas_call(
        functools.partial(_fnet_stage1_kernel, groups=FNET_GROUPS),
        grid=(b, n2 // nb),
        in_specs=[blk, full(tables["cs"]), m1_spec],
        out_specs=[blk, blk],
        out_shape=[jax.ShapeDtypeStruct((b, n1, n2, width), BF16)] * 2,
        compiler_params=_params("parallel", "parallel"),
        name="fnet_stage1",
    )(x4, tables["cs"], tables["m1"])
    blk2 = pl.BlockSpec((None, kb, n2, width), lambda bi, j: (bi, j, 0, 0))
    z4 = z.reshape(b, n2, n1, z.shape[-1])
    scale = 1.0 / math.sqrt(l * gw)
    f = pl.pallas_call(
        functools.partial(_fnet_stage2_kernel, scale=scale),
        grid=(b, n1 // kb),
        in_specs=[blk2, blk2, full(tables["m2"]),
                  pl.BlockSpec((None, n2, kb, width), lambda bi, j: (bi, 0, j, col_s))],
        out_specs=pl.BlockSpec((None, n2, kb, width), lambda bi, j: (bi, 0, j, 0)),
        out_shape=jax.ShapeDtypeStruct((b, n2, n1, width), BF16),
        compiler_params=_params("parallel", "parallel"),
        name="fnet_stage2",
    )(ar, ai, tables["m2"], z4)
    return f.reshape(b, l, width)


def _branch_kernel(pa, pb, pc, pd, w_ref, ga, gb, gc, gd, o_ref):
    acc = None
    for k, (p, g) in enumerate(((pa, ga), (pb, gb), (pc, gc), (pd, gd))):
        br = jnp.dot(p[...], w_ref[k], preferred_element_type=F32)
        term = _sigmoid(g[...].astype(F32)) * br
        acc = term if acc is None else acc + term
    o_ref[...] = acc.astype(o_ref.dtype)


def _branch_merge(ps, w_branch, layer, z2d, gate_col0, d_model):
    m, width = ps[0].shape
    tm = _row_tile(m, 1024)
    tn = _row_tile(d_model, 1024)
    assert gate_col0 % tn == 0
    g0 = gate_col0 // tn
    per = d_model // tn
    p_spec = pl.BlockSpec((tm, width), lambda i, j: (i, 0))
    g_specs = [pl.BlockSpec((tm, tn), functools.partial(lambda i, j, k: (i, g0 + k * per + j), k=k))
               for k in range(N_BRANCH)]
    return pl.pallas_call(
        _branch_kernel,
        grid=(m // tm, d_model // tn),
        in_specs=[p_spec] * 4 + [pl.BlockSpec((None, N_BRANCH, width, tn), lambda i, j: (layer, 0, 0, j))] + g_specs,
        out_specs=pl.BlockSpec((tm, tn), lambda i, j: (i, j)),
        out_shape=jax.ShapeDtypeStruct((m, d_model), BF16),
        compiler_params=_params("parallel", "parallel"),
        name="branch_merge",
    )(*ps, w_branch, z2d, z2d, z2d, z2d)


def _out_norm_kernel(*refs, final):
    if final:
        m_ref, w_ref, x_ref, gate_ref, g_ref, h_ref, xrow_ref = refs
        o_ref = sc_ref = sh_ref = None
    else:
        m_ref, w_ref, x_ref, gate_ref, g_ref, sc_ref, sh_ref, o_ref, h_ref, xrow_ref = refs
    j = pl.program_id(2)
    nj = xrow_ref.shape[0] + 1
    tn = x_ref.shape[-1]

    def residual():
        xn = x_ref[...] + gate_ref[...] * jnp.dot(m_ref[...], w_ref[...], preferred_element_type=F32)
        if o_ref is not None:
            o_ref[...] = xn
        return xn

    @pl.when(j < nj - 1)
    def _():
        xrow_ref[j] = residual()

    @pl.when(j == nj - 1)
    def _():
        parts = [xrow_ref[jj] for jj in range(nj - 1)] + [residual()]
        ss = parts[0] * parts[0]
        for p in parts[1:]:
            ss = ss + p * p
        rstd = lax.rsqrt(jnp.sum(ss, axis=-1, keepdims=True) / (nj * tn) + EPS)
        for jj, p in enumerate(parts):
            sl = slice(jj * tn, (jj + 1) * tn)
            n = p * rstd * g_ref[:, sl]
            if not final:
                n = n * (1.0 + sc_ref[:, sl]) + sh_ref[:, sl]
            h_ref[:, sl] = n.astype(h_ref.dtype)


def _out_proj_residual_norm(merged, w_out, layer, x, gate, g_next, scale_next=None, shift_next=None):
    b, l, d = x.shape
    final = scale_next is None
    tm = _row_tile(l, 512)
    tn = _row_tile(d, 1024)
    nj = d // tn
    assert nj >= 2
    row = lambda bi, i, j: (bi, i, 0)
    vec = lambda bi, i, j: (bi, 0, 0)
    tile = lambda bi, i, j: (bi, i, j)
    in_specs = [pl.BlockSpec((None, tm, d), row),
                pl.BlockSpec((None, d, tn), lambda bi, i, j: (layer, 0, j)),
                pl.BlockSpec((None, tm, tn), tile),
                pl.BlockSpec((None, 1, tn), lambda bi, i, j: (bi, 0, j)),
                pl.BlockSpec((1, d), lambda bi, i, j: (0, 0))]
    args = [merged, w_out, x, gate, g_next]
    if final:
        out_specs = pl.BlockSpec((None, tm, d), row)
        out_shape = jax.ShapeDtypeStruct((b, l, d), x.dtype)
    else:
        in_specs += [pl.BlockSpec((None, 1, d), vec), pl.BlockSpec((None, 1, d), vec)]
        args += [scale_next, shift_next]
        out_specs = [pl.BlockSpec((None, tm, tn), tile), pl.BlockSpec((None, tm, d), row)]
        out_shape = [jax.ShapeDtypeStruct((b, l, d), x.dtype), jax.ShapeDtypeStruct((b, l, d), BF16)]
    return pl.pallas_call(
        functools.partial(_out_norm_kernel, final=final),
        grid=(b, l // tm, nj),
        in_specs=in_specs,
        out_specs=out_specs,
        out_shape=out_shape,
        scratch_shapes=[pltpu.VMEM((nj - 1, tm, tn), F32)],
        compiler_params=_params("parallel", "parallel", "arbitrary"),
        name="out_proj_final_norm" if final else "out_proj_residual_norm",
    )(*args)


def _block_diag_gates(wa, wx, cb):
    depth, _, heads, hd, _ = wa.shape
    hpb = cb // hd
    n_cb = heads // hpb
    eye = jnp.eye(hpb, dtype=wa.dtype)

    def bd(w):
        w = w.reshape(depth, 2, n_cb, hpb, hd, hd)
        full = jnp.einsum("ldchij,hg->ldchigj", w, eye)
        return full.reshape(depth, 2, n_cb, cb, cb)

    both = 0.5 * jnp.concatenate([bd(wa), bd(wx)], axis=-1)
    return both.transpose(0, 2, 1, 3, 4).astype(BF16)


def _token_mix(z, zd, y_a, width, d_model, lp, tables):
    b, l, n_main = z.shape
    y_b, y_c = _local_mixers(z, lp["cf_conv_w"], lp["cf_prm"], lp["sc_conv_w"], width, (1, 2, 7), (3, 4, 5, 8))
    y_d = _fnet_mixer(zd, z, 9, tables)
    m = b * l
    ps = [y.reshape(m, width) for y in (y_a, y_b, y_c, y_d)]
    merged = _branch_merge(ps, lp["w_branch"], lp["layer"], z.reshape(m, n_main), 10 * width, d_model)
    return merged.reshape(b, l, d_model)


def kernel(x, c, ctx, c_ctx, norm_g, w_mod, b_mod, w_in, lru_conv_w, lru_conv_b, lru_wa, lru_ba, lru_wx,
           lru_bx, lru_lambda, cf_conv_w, cf_conv_b, cf_ln_g, cf_ln_b, sc_conv_w, w_branch, w_out, final_g):
    bsz, seq, d_model = x.shape
    ctx_len = ctx.shape[1]
    depth = w_in.shape[0]
    width = d_model // N_BRANCH
    assert bsz + 1 <= 8

    four_blk = 6
    n_main_blk = w_in.shape[-1] // width - 1
    n_main = n_main_blk * width
    w_branch_b = w_branch.astype(BF16)
    w_out_b = w_out.astype(BF16)

    cb = min(256, width)
    wg = _block_diag_gates(lru_wa, lru_wx, cb)
    lru_prm = jnp.concatenate([lru_conv_w, lru_conv_b[:, :, None], 0.5 * lru_ba[:, :, None],
                               0.5 * lru_bx[:, :, None], lru_lambda[:, :, None]], axis=2).astype(F32)
    cf_prm = jnp.pad(jnp.stack([cf_conv_b, cf_ln_g, cf_ln_b], axis=1), ((0, 0), (0, 5), (0, 0)))
    tables_lat = _fnet_tables(seq, width // FNET_GROUPS)
    tables_ctx = _fnet_tables(ctx_len, width // FNET_GROUPS)

    cin = jnp.concatenate([c, c_ctx[None], jnp.zeros((8 - bsz - 1, d_model), c.dtype)], axis=0)
    mods = _adaln_mod(cin, w_mod, b_mod)

    xc = ctx
    s_blk = (6 * width) // cb
    rows_c = jnp.full((bsz,), bsz, jnp.int32)

    def mod_parts(l):
        shift, scale, gate = (mods[l, :, k * d_model:(k + 1) * d_model] for k in range(3))
        return dict(g=norm_g[l][None], gate=gate[:bsz, None], gate_c=gate[rows_c][:, None],
                    scale=scale[:bsz, None], shift=shift[:bsz, None],
                    scale_c=scale[rows_c][:, None], shift_c=shift[rows_c][:, None])

    mp = mod_parts(0)
    hc = _norm_mod(xc, mp["g"], mp["scale_c"], mp["shift_c"])
    h = _norm_mod(x, mp["g"], mp["scale"], mp["shift"])
    for l in range(depth):
        last = l == depth - 1
        lp = dict(cf_conv_w=cf_conv_w[l], cf_prm=cf_prm[l], sc_conv_w=sc_conv_w[l], w_branch=w_branch_b, layer=l)

        hc2 = hc.reshape(bsz * ctx_len, d_model)
        zc = _in_proj(hc2, w_in, l, 0, 1 if last else n_main_blk, four_blk, width, "in_proj_ctx")
        zc = zc.reshape(bsz, ctx_len, -1)
        h0 = jnp.zeros((bsz, 2, width), F32)
        yc_a, h_ctx = _lru_mixer(zc, s_blk, wg[l], lru_prm[l], h0, width, with_out=not last)

        h2 = h.reshape(bsz * seq, d_model)
        z = _in_proj(h2, w_in, l, 0, n_main_blk, four_blk, width, "in_proj").reshape(bsz, seq, n_main)
        zd = _in_proj(h2, w_in, l, four_blk, 1, None, width, "in_proj_fourier").reshape(bsz, seq, width)
        y_a, _ = _lru_mixer(z, s_blk, wg[l], lru_prm[l], h_ctx, width, with_out=True)
        merged = _token_mix(z, zd, y_a, width, d_model, lp, tables_lat)
        if last:
            return _out_proj_residual_norm(merged, w_out_b, l, x, mp["gate"], final_g[None])
        else:
            zcd = _in_proj(hc2, w_in, l, four_blk, 1, None, width, "in_proj_fourier_ctx")
            zcd = zcd.reshape(bsz, ctx_len, width)
            merged_c = _token_mix(zc, zcd, yc_a, width, d_model, lp, tables_ctx)
            nxt = mod_parts(l + 1)
            xc, hc = _out_proj_residual_norm(merged_c, w_out_b, l, xc, mp["gate_c"],
                                             nxt["g"], nxt["scale_c"], nxt["shift_c"])
            x, h = _out_proj_residual_norm(merged, w_out_b, l, x, mp["gate"],
                                           nxt["g"], nxt["scale"], nxt["shift"])
            mp = nxt
```

```python
import functools
import math

import numpy as np
import jax
import jax.numpy as jnp
from jax import lax
from jax.experimental import pallas as pl
from jax.experimental.pallas import tpu as pltpu

EPS = 1e-6
LRU_C = 8.0
N_BRANCH = 4
FNET_GROUPS = 4
V7X_VMEM_LIMIT_BYTES = 60 * 1024 * 1024
BF16_SUBLANES = 16
_LANES = 128
F32 = jnp.float32
BF16 = jnp.bfloat16


def _params(*sem):
    return pltpu.CompilerParams(dimension_semantics=sem, vmem_limit_bytes=V7X_VMEM_LIMIT_BYTES)


def _sigmoid(v):
    return 0.5 * jnp.tanh(0.5 * v) + 0.5


def _silu(v):
    return v * _sigmoid(v)


def _mod_kernel(c_ref, w_ref, b_ref, o_ref):
    s = _silu(c_ref[...]).astype(BF16)
    w = w_ref[0].astype(BF16)
    o_ref[0] = jnp.dot(s, w, preferred_element_type=F32) + b_ref[0]


def _adaln_mod(cin, w_mod, b_mod):
    depth, d, n = w_mod.shape
    tn = min(512, n)
    return pl.pallas_call(
        _mod_kernel,
        grid=(depth, n // tn),
        in_specs=[pl.BlockSpec((8, d), lambda l, j: (0, 0)),
                  pl.BlockSpec((1, d, tn), lambda l, j: (l, 0, j)),
                  pl.BlockSpec((1, 1, tn), lambda l, j: (l, 0, j))],
        out_specs=pl.BlockSpec((1, 8, tn), lambda l, j: (l, 0, j)),
        out_shape=jax.ShapeDtypeStruct((depth, 8, n), F32),
        compiler_params=_params("parallel", "parallel"),
        name="adaln_mod",
    )(cin, w_mod, b_mod.reshape(depth, 1, n))


def _norm_mod_kernel(x_ref, g_ref, sc_ref, sh_ref, o_ref):
    x = x_ref[...]
    n = x * lax.rsqrt(jnp.mean(x * x, axis=-1, keepdims=True) + EPS) * g_ref[...]
    o_ref[...] = (n * (1.0 + sc_ref[...]) + sh_ref[...]).astype(o_ref.dtype)


def _row_tile(l, cap):
    t = min(l, cap)
    while l % t:
        t //= 2
    return t


def _norm_mod(x, g, scale, shift):
    b, l, d = x.shape
    tr = _row_tile(l, 256)
    return pl.pallas_call(
        _norm_mod_kernel,
        grid=(b, l // tr),
        in_specs=[pl.BlockSpec((None, tr, d), lambda bi, i: (bi, i, 0)),
                  pl.BlockSpec((1, d), lambda bi, i: (0, 0)),
                  pl.BlockSpec((None, 1, d), lambda bi, i: (bi, 0, 0)),
                  pl.BlockSpec((None, 1, d), lambda bi, i: (bi, 0, 0))],
        out_specs=pl.BlockSpec((None, tr, d), lambda bi, i: (bi, i, 0)),
        out_shape=jax.ShapeDtypeStruct((b, l, d), BF16),
        compiler_params=_params("parallel", "parallel"),
        name="norm_mod",
    )(x, g, scale, shift)


def _in_proj_kernel(h_ref, w_hbm, o_ref, wb_ref, wf_ref, sem, *, layer, first_blk, skip_blk):
    j = pl.program_id(0)
    i = pl.program_id(1)
    tn = wb_ref.shape[-1]
    n_slots = wf_ref.shape[0]

    def panel_copy(jj):
        blk = first_blk + jj
        if skip_blk is not None:
            blk = blk + jnp.where(blk >= skip_blk, 1, 0)
        col = pl.multiple_of(blk * tn, tn)
        slot = jj % n_slots
        return pltpu.make_async_copy(w_hbm.at[layer, :, pl.ds(col, tn)], wf_ref.at[slot], sem.at[slot])

    def start_next():
        @pl.when(j + 1 < pl.num_programs(0))
        def _():
            panel_copy(j + 1).start()

    @pl.when(i == 0)
    def _():
        @pl.when(j == 0)
        def _():
            panel_copy(j).start()

        if n_slots > 1:
            start_next()
        panel_copy(j).wait()
        wb_ref[...] = wf_ref[j % n_slots].astype(BF16)
        if n_slots == 1:
            start_next()

    o_ref[...] = jnp.dot(h_ref[...], wb_ref[...], preferred_element_type=F32).astype(o_ref.dtype)


def _in_proj(h, w_in, layer, first_blk, n_blk, skip_blk, tn, name):
    m, k = h.shape
    tm = _row_tile(m, 1024)
    n_slots = 2 if m == tm else 1
    return pl.pallas_call(
        functools.partial(_in_proj_kernel, layer=layer, first_blk=first_blk, skip_blk=skip_blk),
        grid=(n_blk, m // tm),
        in_specs=[pl.BlockSpec((tm, k), lambda j, i: (i, 0)),
                  pl.BlockSpec(memory_space=pl.ANY)],
        out_specs=[pl.BlockSpec((tm, tn), lambda j, i: (i, j)),
                   pl.BlockSpec((k, tn), lambda j, i: (0, j))],
        out_shape=[jax.ShapeDtypeStruct((m, n_blk * tn), BF16), jax.ShapeDtypeStruct((k, n_blk * tn), BF16)],
        scratch_shapes=[pltpu.VMEM((n_slots, k, tn), F32), pltpu.SemaphoreType.DMA((n_slots,))],
        compiler_params=_params("arbitrary", "arbitrary"),
        name=name,
    )(h, w_in)


def _matmul_kernel(a_ref, b_ref, o_ref):
    o_ref[...] = jnp.dot(a_ref[...], b_ref[...], preferred_element_type=F32).astype(o_ref.dtype)


def _matmul_bf16(a, b, n_cols, name):
    m, k = a.shape
    tn = _row_tile(n_cols, 1024)
    return pl.pallas_call(
        _matmul_kernel,
        grid=(n_cols // tn,),
        in_specs=[pl.BlockSpec((m, k), lambda j: (0, 0)),
                  pl.BlockSpec((k, tn), lambda j: (0, j))],
        out_specs=pl.BlockSpec((m, tn), lambda j: (0, j)),
        out_shape=jax.ShapeDtypeStruct((m, n_cols), BF16),
        compiler_params=_params("parallel"),
        name=name,
    )(a, b)


_LRU_ROW_CONV_B, _LRU_ROW_BA, _LRU_ROW_BX, _LRU_ROW_LAMBDA = 4, 5, 6, 7


def _lru_kernel(*refs, seq, chunk, n_conv, with_out):
    if with_out:
        x_ref, s_ref, wg_ref, prm_ref, h0_ref, y_ref, hfin_ref, xs_ref, hf_ref, hb_ref = refs
    else:
        x_ref, wg_ref, prm_ref, h0_ref, hfin_ref, xs_ref = refs
    cb = x_ref.shape[-1]
    n_lb = cb // _LANES
    n_chunks = seq // chunk
    groups = chunk // 8
    n_gv = groups // 8
    halo = BF16_SUBLANES
    row = lax.broadcasted_iota(jnp.int32, (n_gv, 8, _LANES), 1)

    def gates(u, d):
        prm = prm_ref[d]
        g = jnp.dot(u.astype(BF16), wg_ref[d], preferred_element_type=F32)
        t_r = jnp.tanh(g[:, :cb] + prm[_LRU_ROW_BA:_LRU_ROW_BA + 1])
        t_i = jnp.tanh(g[:, cb:] + prm[_LRU_ROW_BX:_LRU_ROW_BX + 1])
        lam = prm[_LRU_ROW_LAMBDA:_LRU_ROW_LAMBDA + 1]
        c_half = (0.5 * LRU_C) * (jnp.minimum(lam, 0.0) - jnp.log1p(jnp.exp(-jnp.abs(lam))))
        log_a = c_half * t_r + c_half
        a = jnp.exp(log_a)
        th = jnp.tanh(log_a)
        q = (-0.5 * th) / (1.0 - th)
        bb = jnp.sqrt(q) * (u * (t_i + 1.0))
        return a, bb

    def conv(c, d):
        base = pl.multiple_of(c * chunk, chunk)
        cur = x_ref[pl.ds(base, chunk), :].astype(F32)
        if d == 0:
            pbase = pl.multiple_of(jnp.maximum(base - halo, 0), halo)
            edge = x_ref[pl.ds(pbase, halo), :].astype(F32)[halo - 8:] * jnp.where(c > 0, 1.0, 0.0)
            first = -(n_conv - 1)
        else:
            nbase = pl.multiple_of(jnp.minimum(base + chunk, seq - halo), halo)
            edge = x_ref[pl.ds(nbase, halo), :].astype(F32)[:8] * jnp.where(c < n_chunks - 1, 1.0, 0.0)
            first = 0
        prm = prm_ref[d]
        cols = []
        for lb in range(n_lb):
            sl = slice(lb * _LANES, (lb + 1) * _LANES)
            xs_ref[d, lb, 8:8 + chunk] = cur[:, sl]
            if d == 0:
                xs_ref[d, lb, 0:8] = edge[:, sl]
            else:
                xs_ref[d, lb, 8 + chunk:16 + chunk] = edge[:, sl]
            xm = [xs_ref[d, lb, pl.ds(8 + first + m, groups, stride=8), :] for m in range(8 + n_conv - 1)]
            parts = []
            for s in range(8):
                u = prm[_LRU_ROW_CONV_B:_LRU_ROW_CONV_B + 1, sl] + xm[s] * prm[0:1, sl]
                for k in range(1, n_conv):
                    u = u + xm[s + k] * prm[k:k + 1, sl]
                parts.append(u)
            cols.append(jnp.concatenate(parts, axis=0))
        return jnp.concatenate(cols, axis=1)

    def scan(a, bb, carry, c, d, dst_ref):
        base = pl.multiple_of(c * chunk, chunk)
        order = range(8) if d == 0 else range(7, -1, -1)
        new_carry = []
        for lb in range(n_lb):
            sl = slice(lb * _LANES, (lb + 1) * _LANES)
            hs, prods = {}, {}
            h = prod = None
            for s in order:
                a_s = a[s * groups:(s + 1) * groups, sl]
                b_s = bb[s * groups:(s + 1) * groups, sl]
                h = b_s if h is None else a_s * h + b_s
                prod = a_s if prod is None else a_s * prod
                hs[s], prods[s] = h, prod
            a3 = prod.reshape(n_gv, 8, _LANES)
            b3 = h.reshape(n_gv, 8, _LANES)
            for st in (1, 2, 4):
                shift = st if d == 0 else 8 - st
                valid = (row >= st) if d == 0 else (row < 8 - st)
                a_sh = jnp.where(valid, pltpu.roll(a3, shift, axis=1), 1.0)
                b_sh = jnp.where(valid, pltpu.roll(b3, shift, axis=1), 0.0)
                b3 = a3 * b_sh + b3
                a3 = a3 * a_sh
            cur = carry[:, sl]
            cin = [None] * n_gv
            for v in (range(n_gv) if d == 0 else range(n_gv - 1, -1, -1)):
                after = a3[v] * cur + b3[v]
                if d == 0:
                    cin[v] = jnp.where(row[0] == 0, cur, pltpu.roll(after, 1, axis=0))
                    cur = after[7:8]
                else:
                    cin[v] = jnp.where(row[0] == 7, cur, pltpu.roll(after, 7, axis=0))
                    cur = after[0:1]
            new_carry.append(cur)
            if dst_ref is not None:
                cin = jnp.concatenate(cin, axis=0)
                for s in order:
                    dst_ref[lb, pl.ds(base + s, groups, stride=8), :] = hs[s] + prods[s] * cin
        return jnp.concatenate(new_carry, axis=1)

    def body(c, carry):
        cf, cbk = carry
        a, bb = gates(conv(c, 0), 0)
        cf = scan(a, bb, cf, c, 0, hf_ref if with_out else None)
        cr = n_chunks - 1 - c
        a, bb = gates(conv(cr, 1), 1)
        cbk = scan(a, bb, cbk, cr, 1, hb_ref if with_out else None)
        return cf, cbk

    h0 = h0_ref[...]
    cf, cbk = lax.fori_loop(0, n_chunks, body, (h0[0:1], h0[1:2]))
    hfin_ref[...] = jnp.concatenate([cf, cbk], axis=0)

    if with_out:
        def finish(c, _):
            base = pl.multiple_of(c * chunk, chunk)
            s = s_ref[pl.ds(base, chunk), :].astype(F32)
            y = jnp.concatenate([hf_ref[lb, pl.ds(base, chunk), :] + hb_ref[lb, pl.ds(base, chunk), :]
                                 for lb in range(n_lb)], axis=1)
            y_ref[pl.ds(base, chunk), :] = (y * _silu(s)).astype(y_ref.dtype)
            return 0
        lax.fori_loop(0, n_chunks, finish, 0)


def _lru_mixer(z, s_col_blk, wg, prm, h0, width, with_out):
    b, l, _ = z.shape
    n_cb, _, cb, _ = wg.shape
    n_conv = 4
    chunk = _row_tile(l, 256)
    kern = functools.partial(_lru_kernel, seq=l, chunk=chunk, n_conv=n_conv, with_out=with_out)
    x_spec = pl.BlockSpec((None, l, cb), lambda bi, ci: (bi, 0, ci))
    s_spec = pl.BlockSpec((None, l, cb), lambda bi, ci: (bi, 0, s_col_blk + ci))
    wg_spec = pl.BlockSpec((None, 2, cb, 2 * cb), lambda bi, ci: (ci, 0, 0, 0))
    prm_spec = pl.BlockSpec((2, 8, cb), lambda bi, ci: (0, 0, ci))
    st_spec = pl.BlockSpec((None, 2, cb), lambda bi, ci: (bi, 0, ci))
    st_shape = jax.ShapeDtypeStruct((b, 2, width), F32)
    n_lb = cb // _LANES
    assert cb % _LANES == 0 and chunk % 64 == 0
    stage = [pltpu.VMEM((2, n_lb, chunk + 16, _LANES), F32)]
    if with_out:
        y, hfin = pl.pallas_call(
            kern, grid=(b, n_cb),
            in_specs=[x_spec, s_spec, wg_spec, prm_spec, st_spec],
            out_specs=[pl.BlockSpec((None, l, cb), lambda bi, ci: (bi, 0, ci)), st_spec],
            out_shape=[jax.ShapeDtypeStruct((b, l, width), BF16), st_shape],
            scratch_shapes=stage + [pltpu.VMEM((n_lb, l, _LANES), F32), pltpu.VMEM((n_lb, l, _LANES), F32)],
            compiler_params=_params("parallel", "parallel"),
            name="rglru_mixer",
        )(z, z, wg, prm, h0)
        return y, hfin
    hfin = pl.pallas_call(
        kern, grid=(b, n_cb),
        in_specs=[x_spec, wg_spec, prm_spec, st_spec],
        out_specs=st_spec,
        out_shape=st_shape,
        scratch_shapes=stage,
        compiler_params=_params("parallel", "parallel"),
        name="rglru_state",
    )(z, wg, prm, h0)
    return None, hfin


_CONV_ROWS = 64
_CONV_SLAB = 96
_EPI_ROWS = 64


def _halo_specs(t, width, col_blk, n_row_blocks):
    r = t // BF16_SUBLANES
    last = n_row_blocks * r - 1
    cur = pl.BlockSpec((None, t, width), lambda bi, i: (bi, i, col_blk))
    prev = pl.BlockSpec((None, BF16_SUBLANES, width), lambda bi, i: (bi, jnp.maximum(i * r - 1, 0), col_blk))
    nxt = pl.BlockSpec((None, BF16_SUBLANES, width), lambda bi, i: (bi, jnp.minimum((i + 1) * r, last), col_blk))
    return prev, cur, nxt


def _fill_window(win_ref, fn, prev_refs, cur_refs, next_refs, t):
    i = pl.program_id(1)
    n = pl.num_programs(1)
    h = BF16_SUBLANES
    load = lambda refs: [r[...].astype(F32) for r in refs]
    win_ref[0:h] = fn(*load(prev_refs)) * jnp.where(i > 0, 1.0, 0.0)
    win_ref[h:h + t] = fn(*load(cur_refs))
    win_ref[h + t:h + t + h] = fn(*load(next_refs)) * jnp.where(i < n - 1, 1.0, 0.0)


def _dw_conv(win_ref, wb_ref, dst_ref, t, n_taps):
    off = BF16_SUBLANES - n_taps // 2
    assert off >= 0 and off + n_taps - 1 + _CONV_ROWS <= _CONV_SLAB
    width = dst_ref.shape[-1]
    groups = _CONV_ROWS // 8

    def tile(ti, _):
        r0 = pl.multiple_of(ti * _CONV_ROWS, _CONV_ROWS)
        for c0 in range(0, width, _LANES):
            slab = win_ref[pl.ds(r0, _CONV_SLAB), c0:c0 + _LANES]
            acc = None
            for res in range(8):
                taps = [o for o in range(off, off + n_taps) if o % 8 == res]
                if not taps:
                    continue
                sh = slab if res == 0 else pltpu.roll(slab, _CONV_SLAB - res, axis=0)
                for o in taps:
                    q = 8 * (o // 8)
                    term = sh[q:q + _CONV_ROWS].reshape(groups, 8, _LANES) * wb_ref[o - off][:, c0:c0 + _LANES]
                    acc = term if acc is None else acc + term
            dst_ref[pl.ds(r0, _CONV_ROWS), c0:c0 + _LANES] = acc.reshape(_CONV_ROWS, _LANES)
        return 0

    lax.fori_loop(0, t // _CONV_ROWS, tile, 0)


def _conformer_kernel(vp, gp, vc, gc, vn, gn, s_ref, wb_ref, prm_ref, o_ref, win_ref, conv_ref, *, t, n_taps):
    _fill_window(win_ref, lambda v, g: v * _sigmoid(g), (vp, gp), (vc, gc), (vn, gn), t)
    _dw_conv(win_ref, wb_ref, conv_ref, t, n_taps)

    def tile(ti, _):
        r0 = pl.multiple_of(ti * _EPI_ROWS, _EPI_ROWS)
        acc = conv_ref[pl.ds(r0, _EPI_ROWS), :] + prm_ref[0:1, :]
        mu = jnp.mean(acc, axis=-1, keepdims=True)
        d = acc - mu
        var = jnp.mean(d * d, axis=-1, keepdims=True)
        yn = d * lax.rsqrt(var + EPS) * prm_ref[1:2, :] + prm_ref[2:3, :]
        s = s_ref[pl.ds(r0, _EPI_ROWS), :].astype(F32)
        o_ref[pl.ds(r0, _EPI_ROWS), :] = (_silu(yn) * _silu(s)).astype(o_ref.dtype)
        return 0

    trips = t // _EPI_ROWS
    lax.fori_loop(0, trips, tile, 0, unroll=2 if trips % 2 == 0 else 1)


def _shortconv_kernel(xp, cp, xc, cc, xn, cn, gb_ref, s_ref, wb_ref, o_ref, win_ref, conv_ref, *, t, n_taps):
    _fill_window(win_ref, lambda x, c: c * x, (xp, cp), (xc, cc), (xn, cn), t)
    _dw_conv(win_ref, wb_ref, conv_ref, t, n_taps)

    def tile(ti, _):
        r0 = pl.multiple_of(ti * _EPI_ROWS, _EPI_ROWS)
        gb = gb_ref[pl.ds(r0, _EPI_ROWS), :].astype(F32)
        s = s_ref[pl.ds(r0, _EPI_ROWS), :].astype(F32)
        o_ref[pl.ds(r0, _EPI_ROWS), :] = (gb * conv_ref[pl.ds(r0, _EPI_ROWS), :] * _silu(s)).astype(o_ref.dtype)
        return 0

    trips = t // _EPI_ROWS
    lax.fori_loop(0, trips, tile, 0, unroll=2 if trips % 2 == 0 else 1)


def _sublane_repeat(w):
    return jnp.broadcast_to(w.astype(F32)[:, None, :], (w.shape[0], 8, w.shape[1]))


def _conv_scratch(t, width):
    return [pltpu.VMEM((t + 2 * BF16_SUBLANES, width), F32), pltpu.VMEM((t, width), F32)]


_N_CONFORMER_IN = 9


def _local_mix_kernel(*refs, t, taps_cf, taps_sc):
    cf_in, sc_in = refs[:_N_CONFORMER_IN], refs[_N_CONFORMER_IN:-4]
    ob_ref, oc_ref, win_ref, conv_ref = refs[-4:]
    _conformer_kernel(*cf_in, ob_ref, win_ref, conv_ref, t=t, n_taps=taps_cf)
    _shortconv_kernel(*sc_in, oc_ref, win_ref, conv_ref, t=t, n_taps=taps_sc)


def _local_mixers(z, w_cf, prm_cf, w_sc, width, cf_cols, sc_cols):
    b, l, _ = z.shape
    t = _row_tile(l, 512)
    nb = l // t
    wb_cf = _sublane_repeat(w_cf)
    wb_sc = _sublane_repeat(w_sc)
    col_val, col_gl, col_sb = cf_cols
    col_x, col_gb, col_gc, col_sc = sc_cols
    vp, vc, vn = _halo_specs(t, width, col_val, nb)
    gp, gc, gn = _halo_specs(t, width, col_gl, nb)
    xp, xc, xn = _halo_specs(t, width, col_x, nb)
    cp, cc, cn = _halo_specs(t, width, col_gc, nb)
    rows = lambda col: pl.BlockSpec((None, t, width), lambda bi, i: (bi, i, col))
    whole = lambda a: pl.BlockSpec(a.shape, lambda bi, i: (0,) * a.ndim)
    cf_specs = [vp, gp, vc, gc, vn, gn, rows(col_sb), whole(wb_cf), whole(prm_cf)]
    sc_specs = [xp, cp, xc, cc, xn, cn, rows(col_gb), rows(col_sc), whole(wb_sc)]
    assert len(cf_specs) == _N_CONFORMER_IN
    out = jax.ShapeDtypeStruct((b, l, width), BF16)
    return pl.pallas_call(
        functools.partial(_local_mix_kernel, t=t, taps_cf=w_cf.shape[0], taps_sc=w_sc.shape[0]),
        grid=(b, nb),
        in_specs=cf_specs + sc_specs,
        out_specs=[rows(0), rows(0)],
        out_shape=[out, out],
        scratch_shapes=_conv_scratch(t, width),
        compiler_params=_params("parallel", "parallel"),
        name="local_mixers",
    )(*([z] * 7), wb_cf, prm_cf, *([z] * 8), wb_sc)


def _dft_split(n):
    n2 = 1 << (int(math.log2(n)) // 2)
    n1 = n // n2
    assert n1 * n2 == n
    return n1, n2


def _cos_sin(rows, cols, period):
    ang = 2.0 * np.pi * ((np.outer(np.arange(rows), np.arange(cols))) % period) / period
    return np.cos(ang), np.sin(ang)


def _fnet_tables(seq, group_width):
    n1, n2 = _dft_split(seq)
    cc, sc = _cos_sin(group_width, group_width, group_width)
    c1, s1 = (jnp.asarray(v, F32) for v in _cos_sin(n1, n1, n1))
    c2, s2 = _cos_sin(n2, n2, n2)
    tc, ts = (jnp.asarray(v, F32)[:, :, None] for v in _cos_sin(n2, n1, seq))
    cm = c1[None] * tc - s1[None] * ts
    sm = s1[None] * tc + c1[None] * ts
    m1 = jnp.concatenate([jnp.concatenate([cm, -sm], axis=2), jnp.concatenate([sm, cm], axis=2)], axis=1)
    return dict(
        cs=jnp.asarray(np.concatenate([cc, sc], axis=1), F32).astype(BF16),
        m1=m1.astype(BF16),
        m2=jnp.asarray(np.concatenate([c2, -s2], axis=1), F32).astype(BF16),
    )


def _fnet_stage1_kernel(x_ref, cs_ref, m1_ref, or_ref, oi_ref, *, groups):
    n1, nb, width = x_ref.shape
    gw = width // groups
    x = pltpu.einshape("abc->bac", x_ref[...]).reshape(nb * n1, width)
    ps, qs = [], []
    for g in range(groups):
        pq = jnp.dot(x[:, g * gw:(g + 1) * gw], cs_ref[...], preferred_element_type=F32)
        ps.append(pq[:, :gw])
        qs.append(pq[:, gw:])
    p = jnp.concatenate(ps, axis=1).astype(BF16)
    q = jnp.concatenate(qs, axis=1).astype(BF16)
    res_r, res_i = [], []
    for j in range(nb):
        pq = jnp.concatenate([p[j * n1:(j + 1) * n1], q[j * n1:(j + 1) * n1]], axis=0)
        a = jnp.dot(m1_ref[j], pq, preferred_element_type=F32)
        res_r.append(a[:n1].astype(or_ref.dtype))
        res_i.append(a[n1:].astype(oi_ref.dtype))
    or_ref[...] = pltpu.einshape("abc->bac", jnp.stack(res_r, axis=0))
    oi_ref[...] = pltpu.einshape("abc->bac", jnp.stack(res_i, axis=0))


def _fnet_stage2_kernel(ar_ref, ai_ref, m2_ref, s_ref, o_ref, *, scale):
    kb = ar_ref.shape[0]
    res = []
    for j in range(kb):
        rhs = jnp.concatenate([ar_ref[j], ai_ref[j]], axis=0)
        res.append(jnp.dot(m2_ref[...], rhs, preferred_element_type=F32) * scale)
    f = pltpu.einshape("abc->bac", jnp.stack(res, axis=0))
    o_ref[...] = (f * _silu(s_ref[...].astype(F32))).astype(o_ref.dtype)


def _fnet_mixer(zd, z, col_s, tables):
    b, l, width = zd.shape
    n1, n2 = _dft_split(l)
    gw = width // FNET_GROUPS
    nb = min(n2, BF16_SUBLANES)
    kb = min(n1, BF16_SUBLANES)
    x4 = zd.reshape(b, n1, n2, width)
    blk = pl.BlockSpec((None, n1, nb, width), lambda bi, j: (bi, 0, j, 0))
    m1_spec = pl.BlockSpec((nb, 2 * n1, 2 * n1), lambda bi, j: (j, 0, 0))
    full = lambda a: pl.BlockSpec(a.shape, lambda bi, j: (0,) * a.ndim)
    ar, ai = pl.pallas_call(
        functools.partial(_fnet_stage1_kernel, groups=FNET_GROUPS),
        grid=(b, n2 // nb),
        in_specs=[blk, full(tables["cs"]), m1_spec],
        out_specs=[blk, blk],
        out_shape=[jax.ShapeDtypeStruct((b, n1, n2, width), BF16)] * 2,
        compiler_params=_params("parallel", "parallel"),
        name="fnet_stage1",
    )(x4, tables["cs"], tables["m1"])
    blk2 = pl.BlockSpec((None, kb, n2, width), lambda bi, j: (bi, j, 0, 0))
    z4 = z.reshape(b, n2, n1, z.shape[-1])
    scale = 1.0 / math.sqrt(l * gw)
    f = pl.pallas_call(
        functools.partial(_fnet_stage2_kernel, scale=scale),
        grid=(b, n1 // kb),
        in_specs=[blk2, blk2, full(tables["m2"]),
                  pl.BlockSpec((None, n2, kb, width), lambda bi, j: (bi, 0, j, col_s))],
        out_specs=pl.BlockSpec((None, n2, kb, width), lambda bi, j: (bi, 0, j, 0)),
        out_shape=jax.ShapeDtypeStruct((b, n2, n1, width), BF16),
        compiler_params=_params("parallel", "parallel"),
        name="fnet_stage2",
    )(ar, ai, tables["m2"], z4)
    return f.reshape(b, l, width)


def _branch_kernel(pa, pb, pc, pd, w_ref, ga, gb, gc, gd, o_ref):
    acc = None
    for k, (p, g) in enumerate(((pa, ga), (pb, gb), (pc, gc), (pd, gd))):
        br = jnp.dot(p[...], w_ref[k], preferred_element_type=F32)
        term = _sigmoid(g[...].astype(F32)) * br
        acc = term if acc is None else acc + term
    o_ref[...] = acc.astype(o_ref.dtype)


def _branch_merge(ps, w_branch, layer, z2d, gate_col0, d_model):
    m, width = ps[0].shape
    tm = _row_tile(m, 1024)
    tn = _row_tile(d_model, 1024)
    assert gate_col0 % tn == 0
    g0 = gate_col0 // tn
    per = d_model // tn
    p_spec = pl.BlockSpec((tm, width), lambda i, j: (i, 0))
    g_specs = [pl.BlockSpec((tm, tn), functools.partial(lambda i, j, k: (i, g0 + k * per + j), k=k))
               for k in range(N_BRANCH)]
    return pl.pallas_call(
        _branch_kernel,
        grid=(m // tm, d_model // tn),
        in_specs=[p_spec] * 4 + [pl.BlockSpec((None, N_BRANCH, width, tn), lambda i, j: (layer, 0, 0, j))] + g_specs,
        out_specs=pl.BlockSpec((tm, tn), lambda i, j: (i, j)),
        out_shape=jax.ShapeDtypeStruct((m, d_model), BF16),
        compiler_params=_params("parallel", "parallel"),
        name="branch_merge",
    )(*ps, w_branch, z2d, z2d, z2d, z2d)


def _out_norm_kernel(*refs, final):
    if final:
        m_ref, w_ref, x_ref, gate_ref, g_ref, h_ref, xrow_ref = refs
        o_ref = sc_ref = sh_ref = None
    else:
        m_ref, w_ref, x_ref, gate_ref, g_ref, sc_ref, sh_ref, o_ref, h_ref, xrow_ref = refs
    j = pl.program_id(2)
    nj = xrow_ref.shape[0] + 1
    tn = x_ref.shape[-1]

    def residual():
        xn = x_ref[...] + gate_ref[...] * jnp.dot(m_ref[...], w_ref[...], preferred_element_type=F32)
        if o_ref is not None:
            o_ref[...] = xn
        return xn

    @pl.when(j < nj - 1)
    def _():
        xrow_ref[j] = residual()

    @pl.when(j == nj - 1)
    def _():
        parts = [xrow_ref[jj] for jj in range(nj - 1)] + [residual()]
        ss = parts[0] * parts[0]
        for p in parts[1:]:
            ss = ss + p * p
        rstd = lax.rsqrt(jnp.sum(ss, axis=-1, keepdims=True) / (nj * tn) + EPS)
        for jj, p in enumerate(parts):
            sl = slice(jj * tn, (jj + 1) * tn)
            n = p * rstd * g_ref[:, sl]
            if not final:
                n = n * (1.0 + sc_ref[:, sl]) + sh_ref[:, sl]
            h_ref[:, sl] = n.astype(h_ref.dtype)


def _out_proj_residual_norm(merged, w_out, layer, x, gate, g_next, scale_next=None, shift_next=None):
    b, l, d = x.shape
    final = scale_next is None
    tm = _row_tile(l, 512)
    tn = _row_tile(d, 1024)
    nj = d // tn
    assert nj >= 2
    row = lambda bi, i, j: (bi, i, 0)
    vec = lambda bi, i, j: (bi, 0, 0)
    tile = lambda bi, i, j: (bi, i, j)
    in_specs = [pl.BlockSpec((None, tm, d), row),
                pl.BlockSpec((None, d, tn), lambda bi, i, j: (layer, 0, j)),
                pl.BlockSpec((None, tm, tn), tile),
                pl.BlockSpec((None, 1, tn), lambda bi, i, j: (bi, 0, j)),
                pl.BlockSpec((1, d), lambda bi, i, j: (0, 0))]
    args = [merged, w_out, x, gate, g_next]
    if final:
        out_specs = pl.BlockSpec((None, tm, d), row)
        out_shape = jax.ShapeDtypeStruct((b, l, d), x.dtype)
    else:
        in_specs += [pl.BlockSpec((None, 1, d), vec), pl.BlockSpec((None, 1, d), vec)]
        args += [scale_next, shift_next]
        out_specs = [pl.BlockSpec((None, tm, tn), tile), pl.BlockSpec((None, tm, d), row)]
        out_shape = [jax.ShapeDtypeStruct((b, l, d), x.dtype), jax.ShapeDtypeStruct((b, l, d), BF16)]
    return pl.pallas_call(
        functools.partial(_out_norm_kernel, final=final),
        grid=(b, l // tm, nj),
        in_specs=in_specs,
        out_specs=out_specs,
        out_shape=out_shape,
        scratch_shapes=[pltpu.VMEM((nj - 1, tm, tn), F32)],
        compiler_params=_params("parallel", "parallel", "arbitrary"),
        name="out_proj_final_norm" if final else "out_proj_residual_norm",
    )(*args)


def _block_diag_gates(wa, wx, cb):
    depth, _, heads, hd, _ = wa.shape
    hpb = cb // hd
    n_cb = heads // hpb
    eye = jnp.eye(hpb, dtype=wa.dtype)

    def bd(w):
        w = w.reshape(depth, 2, n_cb, hpb, hd, hd)
        full = jnp.einsum("ldchij,hg->ldchigj", w, eye)
        return full.reshape(depth, 2, n_cb, cb, cb)

    both = 0.5 * jnp.concatenate([bd(wa), bd(wx)], axis=-1)
    return both.transpose(0, 2, 1, 3, 4).astype(BF16)


def _token_mix(z, zd, y_a, width, d_model, lp, tables):
    b, l, n_main = z.shape
    y_b, y_c = _local_mixers(z, lp["cf_conv_w"], lp["cf_prm"], lp["sc_conv_w"], width, (1, 2, 7), (3, 4, 5, 8))
    y_d = _fnet_mixer(zd, z, 9, tables)
    m = b * l
    ps = [y.reshape(m, width) for y in (y_a, y_b, y_c, y_d)]
    merged = _branch_merge(ps, lp["w_branch"], lp["layer"], z.reshape(m, n_main), 10 * width, d_model)
    return merged.reshape(b, l, d_model)


def kernel(x, c, ctx, c_ctx, norm_g, w_mod, b_mod, w_in, lru_conv_w, lru_conv_b, lru_wa, lru_ba, lru_wx,
           lru_bx, lru_lambda, cf_conv_w, cf_conv_b, cf_ln_g, cf_ln_b, sc_conv_w, w_branch, w_out, final_g):
    bsz, seq, d_model = x.shape
    ctx_len = ctx.shape[1]
    depth = w_in.shape[0]
    width = d_model // N_BRANCH
    assert bsz + 1 <= 8

    four_blk = 6
    n_main_blk = w_in.shape[-1] // width - 1
    n_main = n_main_blk * width
    w_branch_b = w_branch.astype(BF16)
    w_out_b = w_out.astype(BF16)

    cb = min(256, width)
    wg = _block_diag_gates(lru_wa, lru_wx, cb)
    lru_prm = jnp.concatenate([lru_conv_w, lru_conv_b[:, :, None], 0.5 * lru_ba[:, :, None],
                               0.5 * lru_bx[:, :, None], lru_lambda[:, :, None]], axis=2).astype(F32)
    cf_prm = jnp.pad(jnp.stack([cf_conv_b, cf_ln_g, cf_ln_b], axis=1), ((0, 0), (0, 5), (0, 0)))
    tables_lat = _fnet_tables(seq, width // FNET_GROUPS)
    tables_ctx = _fnet_tables(ctx_len, width // FNET_GROUPS)

    cin = jnp.concatenate([c, c_ctx[None], jnp.zeros((8 - bsz - 1, d_model), c.dtype)], axis=0)
    mods = _adaln_mod(cin, w_mod, b_mod)

    xc = ctx
    s_blk = (6 * width) // cb
    rows_c = jnp.full((bsz,), bsz, jnp.int32)

    def mod_parts(l):
        shift, scale, gate = (mods[l, :, k * d_model:(k + 1) * d_model] for k in range(3))
        return dict(g=norm_g[l][None], gate=gate[:bsz, None], gate_c=gate[rows_c][:, None],
                    scale=scale[:bsz, None], shift=shift[:bsz, None],
                    scale_c=scale[rows_c][:, None], shift_c=shift[rows_c][:, None])

    mp = mod_parts(0)
    hc = _norm_mod(xc, mp["g"], mp["scale_c"], mp["shift_c"])
    h = _norm_mod(x, mp["g"], mp["scale"], mp["shift"])
    for l in range(depth):
        last = l == depth - 1
        lp = dict(cf_conv_w=cf_conv_w[l], cf_prm=cf_prm[l], sc_conv_w=sc_conv_w[l], w_branch=w_branch_b, layer=l)

        h2 = h.reshape(bsz * seq, d_model)
        z, w_main_b = _in_proj(h2, w_in, l, 0, n_main_blk, four_blk, width, "in_proj")
        zd, w_four_b = _in_proj(h2, w_in, l, four_blk, 1, None, width, "in_proj_fourier")
        z = z.reshape(bsz, seq, n_main)
        zd = zd.reshape(bsz, seq, width)

        hc2 = hc.reshape(bsz * ctx_len, d_model)
        zc = _matmul_bf16(hc2, w_main_b, width if last else n_main, "in_proj_ctx").reshape(bsz, ctx_len, -1)
        h0 = jnp.zeros((bsz, 2, width), F32)
        yc_a, h_ctx = _lru_mixer(zc, s_blk, wg[l], lru_prm[l], h0, width, with_out=not last)

        y_a, _ = _lru_mixer(z, s_blk, wg[l], lru_prm[l], h_ctx, width, with_out=True)
        merged = _token_mix(z, zd, y_a, width, d_model, lp, tables_lat)
        if last:
            return _out_proj_residual_norm(merged, w_out_b, l, x, mp["gate"], final_g[None])
        else:
            zcd = _matmul_bf16(hc2, w_four_b, width, "in_proj_fourier_ctx").reshape(bsz, ctx_len, width)
            merged_c = _token_mix(zc, zcd, yc_a, width, d_model, lp, tables_ctx)
            nxt = mod_parts(l + 1)
            xc, hc = _out_proj_residual_norm(merged_c, w_out_b, l, xc, mp["gate_c"],
                                             nxt["g"], nxt["scale_c"], nxt["shift_c"])
            x, h = _out_proj_residual_norm(merged, w_out_b, l, x, mp["gate"],
                                           nxt["g"], nxt["scale"], nxt["shift"])
            mp = nxt
```
